```python
import jax, jax.numpy as jnp
from jax import lax
import numpy as np

D_MODEL = 1024
BATCH = 16
SEQ = 4096
DEPTH = 4
DEC_BATCH = 8
DEC_SEQ = 16
PAST_LEN = 4096

CHUNK = 64
BLOCK_Q = 128
HEAD_DIM = 128
H_SB = 6
H_FOX = 6
H_MEM = 4
W_SB = H_SB * HEAD_DIM
W_FOX = H_FOX * HEAD_DIM
W_MEM = H_MEM * HEAD_DIM
N_MEM = 256
N_BRANCH = 3
N_IN = 4 * W_SB + 4 * W_FOX + H_FOX + 2 * W_MEM + N_BRANCH * D_MODEL
DN_ALPHA = (2 * DEPTH) ** 0.25
DN_BETA = (8 * DEPTH) ** -0.25
LN_EPS = 1e-5
FORGET_BIAS_LO = 1.0
FORGET_BIAS_HI = 5.0

kernel_name = "stickbreak_fox_memory_hybrid_step"


def _split_cols(h):
    sizes = [W_SB] * 4 + [W_FOX] * 4 + [H_FOX] + [W_MEM] * 2 + [D_MODEL] * N_BRANCH
    idx, run = [], 0
    for s in sizes[:-1]:
        run += s
        idx.append(run)
    return jnp.split(h, idx, axis=-1)


def _layer_norm(x, g, b):
    x32 = x.astype(jnp.float32)
    mu = jnp.mean(x32, axis=-1, keepdims=True)
    var = jnp.mean(jnp.square(x32 - mu), axis=-1, keepdims=True)
    y = (x32 - mu) * lax.rsqrt(var + LN_EPS) * g.astype(jnp.float32) + b.astype(jnp.float32)
    return y.astype(x.dtype)


def _heads(t, h):
    return t.reshape(t.shape[0], t.shape[1], h, HEAD_DIM)


def _stick_breaking(q, k, v, q_pos, k_pos):
    z = jnp.einsum('bthd,bshd->bhts', q, k).astype(jnp.float32) * (HEAD_DIM ** -0.5)
    mask = k_pos[None, :] < q_pos[:, None]
    log1m = jnp.where(mask, jax.nn.log_sigmoid(-z), 0.0)
    between = lax.cumsum(log1m, axis=3, reverse=True) - log1m
    w = jnp.where(mask, jnp.exp(jax.nn.log_sigmoid(z) + between), 0.0)
    return jnp.einsum('bhts,bshd->bthd', w.astype(v.dtype), v)


def _forgetting(q, k, v, fq, fk, q_pos, k_pos):
    z = jnp.einsum('bthd,bshd->bhts', q, k).astype(jnp.float32) * (HEAD_DIM ** -0.5)
    z = z + jnp.transpose(fq, (0, 2, 1))[:, :, :, None] - jnp.transpose(fk, (0, 2, 1))[:, :, None, :]
    mask = k_pos[None, :] <= q_pos[:, None]
    p = jax.nn.softmax(jnp.where(mask, z, -jnp.inf), axis=-1)
    return jnp.einsum('bhts,bshd->bthd', p.astype(v.dtype), v)


def _memory_attn(q, mk, mv):
    z = jnp.einsum('bthd,bmhd->bhtm', q, mk).astype(jnp.float32) * (HEAD_DIM ** -0.5)
    p = jax.nn.softmax(z, axis=-1)
    return jnp.einsum('bhtm,bmhd->bthd', p.astype(mv.dtype), mv)


def _project(x, w_in_l, b_f_l):
    (q_sb, k_sb, v_sb, z_sb, q_fx, k_fx, v_fx, z_fx, f_fx,
     q_mem, z_mem, g_sb, g_fx, g_mem) = _split_cols(x @ w_in_l)
    logf = jax.nn.log_sigmoid(f_fx.astype(jnp.float32) + b_f_l.astype(jnp.float32))
    return (_heads(q_sb, H_SB), _heads(k_sb, H_SB), _heads(v_sb, H_SB), z_sb,
            _heads(q_fx, H_FOX), _heads(k_fx, H_FOX), _heads(v_fx, H_FOX), z_fx, logf,
            _heads(q_mem, H_MEM), z_mem, g_sb, g_fx, g_mem)


def _merge(x, o_sb, z_sb, o_fx, z_fx, o_mem, z_mem, g_sb, g_fx, g_mem,
           w_br_sb_l, w_br_fox_l, w_br_mem_l, w_out_l, ln_g_l, ln_b_l):
    b, t = x.shape[0], x.shape[1]
    y_sb = (o_sb.reshape(b, t, W_SB) * jax.nn.silu(z_sb)) @ w_br_sb_l
    y_fx = (o_fx.reshape(b, t, W_FOX) * jax.nn.silu(z_fx)) @ w_br_fox_l
    y_mem = (o_mem.reshape(b, t, W_MEM) * jax.nn.silu(z_mem)) @ w_br_mem_l
    merged = jax.nn.sigmoid(g_sb) * y_sb + jax.nn.sigmoid(g_fx) * y_fx + jax.nn.sigmoid(g_mem) * y_mem
    return _layer_norm(DN_ALPHA * x + merged @ w_out_l, ln_g_l, ln_b_l)


def _prompt_layer(x, mem, w_in_l, b_f_l, w_mem_kv_l, w_br_sb_l, w_br_fox_l, w_br_mem_l,
                  w_out_l, ln_g_l, ln_b_l):
    (q_sb, k_sb, v_sb, z_sb, q_fx, k_fx, v_fx, z_fx, logf,
     q_mem, z_mem, g_sb, g_fx, g_mem) = _project(x, w_in_l, b_f_l)
    t = x.shape[1]
    pos = jnp.arange(t)
    f_cum = jnp.cumsum(logf, axis=1)
    sb_blocks, fx_blocks = [], []
    for start in range(0, t, BLOCK_Q):
        end = min(start + BLOCK_Q, t)
        sb_blocks.append(_stick_breaking(q_sb[:, start:end], k_sb[:, :end], v_sb[:, :end],
                                         pos[start:end], pos[:end]))
        fx_blocks.append(_forgetting(q_fx[:, start:end], k_fx[:, :end], v_fx[:, :end],
                                     f_cum[:, start:end], f_cum[:, :end], pos[start:end], pos[:end]))
    o_sb = jnp.concatenate(sb_blocks, axis=1)
    o_fx = jnp.concatenate(fx_blocks, axis=1)
    mk, mv = jnp.split(mem @ w_mem_kv_l, 2, axis=-1)
    mk, mv = _heads(mk, H_MEM), _heads(mv, H_MEM)
    o_mem = _memory_attn(q_mem, mk, mv)
    y = _merge(x, o_sb, z_sb, o_fx, z_fx, o_mem, z_mem, g_sb, g_fx, g_mem,
               w_br_sb_l, w_br_fox_l, w_br_mem_l, w_out_l, ln_g_l, ln_b_l)
    return y, (k_sb, v_sb, k_fx, v_fx, logf.astype(x.dtype), mk, mv)


def _sample_layer(x, c_sb_k, c_sb_v, c_fx_k, c_fx_v, c_fx_logf, c_mk, c_mv,
                  w_in_l, b_f_l, w_br_sb_l, w_br_fox_l, w_br_mem_l, w_out_l, ln_g_l, ln_b_l):
    (q_sb, k_sb, v_sb, z_sb, q_fx, k_fx, v_fx, z_fx, logf,
     q_mem, z_mem, g_sb, g_fx, g_mem) = _project(x, w_in_l, b_f_l)
    past, t = c_sb_k.shape[1], x.shape[1]
    q_pos = past + jnp.arange(t)
    k_pos = jnp.arange(past + t)
    o_sb = _stick_breaking(q_sb, jnp.concatenate([c_sb_k, k_sb], axis=1),
                           jnp.concatenate([c_sb_v, v_sb], axis=1), q_pos, k_pos)
    f_cum = jnp.cumsum(jnp.concatenate([c_fx_logf.astype(jnp.float32), logf], axis=1), axis=1)
    o_fx = _forgetting(q_fx, jnp.concatenate([c_fx_k, k_fx], axis=1),
                       jnp.concatenate([c_fx_v, v_fx], axis=1),
                       f_cum[:, past:], f_cum, q_pos, k_pos)
    o_mem = _memory_attn(q_mem, c_mk, c_mv)
    y = _merge(x, o_sb, z_sb, o_fx, z_fx, o_mem, z_mem, g_sb, g_fx, g_mem,
               w_br_sb_l, w_br_fox_l, w_br_mem_l, w_out_l, ln_g_l, ln_b_l)
    return y, (k_sb, v_sb, k_fx, v_fx, logf.astype(x.dtype))


def setup_inputs(seed: int = 0) -> dict:
    key = jax.random.key(seed)
    ks = jax.random.split(key, 24)
    f32 = jnp.float32
    nrm = lambda k, shape: jax.random.normal(k, shape, dtype=f32)
    cache_kv_sb = (DEPTH, DEC_BATCH, PAST_LEN, H_SB, HEAD_DIM)
    cache_kv_fx = (DEPTH, DEC_BATCH, PAST_LEN, H_FOX, HEAD_DIM)
    cache_mem = (DEPTH, DEC_BATCH, N_MEM, H_MEM, HEAD_DIM)
    forget_bias = jnp.linspace(FORGET_BIAS_LO, FORGET_BIAS_HI, H_FOX, dtype=f32)
    return {
        "x_prompt": nrm(ks[0], (BATCH, SEQ, D_MODEL)),
        "x_sample": nrm(ks[1], (DEC_BATCH, DEC_SEQ, D_MODEL)),
        "cache_sb_k": nrm(ks[2], cache_kv_sb),
        "cache_sb_v": nrm(ks[3], cache_kv_sb),
        "cache_fox_k": nrm(ks[4], cache_kv_fx),
        "cache_fox_v": nrm(ks[5], cache_kv_fx),
        "cache_fox_logf": jax.nn.log_sigmoid(3.0 + nrm(ks[6], (DEPTH, DEC_BATCH, PAST_LEN, H_FOX))),
        "cache_mem_k": nrm(ks[7], cache_mem),
        "cache_mem_v": nrm(ks[8], cache_mem),
        "mem_prompt": nrm(ks[9], (BATCH, N_MEM, D_MODEL)),
        "w_in": nrm(ks[10], (DEPTH, D_MODEL, N_IN)) * D_MODEL ** -0.5,
        "b_f": forget_bias[None, :] + 0.1 * nrm(ks[11], (DEPTH, H_FOX)),
        "w_mem_kv": nrm(ks[12], (DEPTH, D_MODEL, 2 * W_MEM)) * D_MODEL ** -0.5,
        "w_br_sb": nrm(ks[13], (DEPTH, W_SB, D_MODEL)) * (W_SB ** -0.5 * DN_BETA),
        "w_br_fox": nrm(ks[14], (DEPTH, W_FOX, D_MODEL)) * (W_FOX ** -0.5 * DN_BETA),
        "w_br_mem": nrm(ks[15], (DEPTH, W_MEM, D_MODEL)) * (W_MEM ** -0.5 * DN_BETA),
        "w_out": nrm(ks[16], (DEPTH, D_MODEL, D_MODEL)) * (D_MODEL ** -0.5 * DN_BETA),
        "ln_g": 1.0 + 0.02 * nrm(ks[17], (DEPTH, D_MODEL)),
        "ln_b": 0.02 * nrm(ks[18], (DEPTH, D_MODEL)),
    }


def reference(x_prompt, x_sample, cache_sb_k, cache_sb_v, cache_fox_k, cache_fox_v,
              cache_fox_logf, cache_mem_k, cache_mem_v, mem_prompt, w_in, b_f, w_mem_kv,
              w_br_sb, w_br_fox, w_br_mem, w_out, ln_g, ln_b):
    xp, xs = x_prompt, x_sample
    p_state = [[] for _ in range(7)]
    s_state = [[] for _ in range(5)]
    for l in range(DEPTH):
        xp, p_new = _prompt_layer(xp, mem_prompt, w_in[l], b_f[l], w_mem_kv[l], w_br_sb[l],
                                  w_br_fox[l], w_br_mem[l], w_out[l], ln_g[l], ln_b[l])
        xs, s_new = _sample_layer(xs, cache_sb_k[l], cache_sb_v[l], cache_fox_k[l], cache_fox_v[l],
                                  cache_fox_logf[l], cache_mem_k[l], cache_mem_v[l],
                                  w_in[l], b_f[l], w_br_sb[l], w_br_fox[l], w_br_mem[l],
                                  w_out[l], ln_g[l], ln_b[l])
        for lst, arr in zip(p_state, p_new):
            lst.append(arr)
        for lst, arr in zip(s_state, s_new):
            lst.append(arr)
    p_sb_k, p_sb_v, p_fox_k, p_fox_v, p_fox_logf, p_mem_k, p_mem_v = [jnp.stack(a, axis=0) for a in p_state]
    s_sb_k, s_sb_v, s_fox_k, s_fox_v, s_fox_logf = [jnp.stack(a, axis=0) for a in s_state]
    return (xp, xs, p_sb_k, p_sb_v, p_fox_k, p_fox_v, p_fox_logf, p_mem_k, p_mem_v,
            s_sb_k, s_sb_v, s_fox_k, s_fox_v, s_fox_logf)
```

```python
import functools

import jax
import jax.numpy as jnp
from jax import lax
from jax.experimental import pallas as pl
from jax.experimental.pallas import tpu as pltpu

F32 = jnp.float32
BF16 = jnp.bfloat16
HEAD_DIM = 128
LANES = 128
SUBLANES = 8
LN_EPS = 1e-5
MASK_VALUE = -1e30
EXP_ZERO_BELOW = -104.0
VMEM_LIMIT_BYTES = 56 * 1024 * 1024

_NT = (((1,), (1,)), ((), ()))


def _params(*sem):
    return pltpu.CompilerParams(dimension_semantics=sem, vmem_limit_bytes=VMEM_LIMIT_BYTES)


def _tile(n, target):
    for t in range(min(n, target), 0, -1):
        if n % t == 0 and t % SUBLANES == 0:
            return t
    return n


def _sigmoid(x):
    return 1.0 / (1.0 + jnp.exp(-x))


def _softplus(x):
    return jnp.maximum(x, 0.0) + jnp.log(1.0 + jnp.exp(-jnp.abs(x)))


def _proj_kernel(x_ref, *refs, kinds, scale):
    w_refs, out_refs = refs[:len(kinds)], refs[len(kinds):]
    x = x_ref[...]
    oi = 0
    for w_ref, kind in zip(w_refs, kinds):
        acc = jnp.dot(x, w_ref[...], preferred_element_type=F32)
        if kind == "kv":
            out_refs[oi][...] = acc
            out_refs[oi + 1][...] = acc.astype(BF16)
            oi += 2
            continue
        if kind == "q":
            acc = acc * scale
        elif kind == "silu":
            acc = acc * _sigmoid(acc)
        elif kind == "sigmoid":
            acc = _sigmoid(acc)
        out_refs[oi][...] = acc.astype(BF16)
        oi += 1


def _project(x_bf, weights, kinds, scale):
    n, d = x_bf.shape
    tm = _tile(n, 512)
    in_specs = [pl.BlockSpec((tm, d), lambda i: (i, 0))]
    out_shape, out_specs = [], []
    for w, kind in zip(weights, kinds):
        in_specs.append(pl.BlockSpec(w.shape, lambda i: (0, 0)))
        for dt in ((F32, BF16) if kind == "kv" else (BF16,)):
            out_shape.append(jax.ShapeDtypeStruct((n, w.shape[1]), dt))
            out_specs.append(pl.BlockSpec((tm, w.shape[1]), lambda i: (i, 0)))
    return pl.pallas_call(
        functools.partial(_proj_kernel, kinds=tuple(kinds), scale=scale),
        grid=(n // tm,), in_specs=in_specs, out_specs=out_specs, out_shape=out_shape,
        compiler_params=_params("parallel"), name="project",
    )(x_bf, *weights)


def _logf_kernel(x_ref, wf_ref, bf_ref, o_ref):
    f = lax.dot_general(wf_ref[...], x_ref[...], _NT, preferred_element_type=F32)
    o_ref[...] = -_softplus(-(f + bf_ref[...]))


def _log_forget(x_bf, wf_t, bf_col, nb):
    n, d = x_bf.shape
    t = n // nb
    tm = _tile(t, 512)
    tpb = t // tm
    return pl.pallas_call(
        _logf_kernel, grid=(nb, tpb),
        in_specs=[pl.BlockSpec((tm, d), lambda b, j: (b * tpb + j, 0)),
                  pl.BlockSpec(wf_t.shape, lambda b, j: (0, 0)),
                  pl.BlockSpec(bf_col.shape, lambda b, j: (0, 0))],
        out_specs=pl.BlockSpec((None, SUBLANES, tm), lambda b, j: (b, 0, j)),
        out_shape=jax.ShapeDtypeStruct((nb, SUBLANES, t), F32),
        compiler_params=_params("parallel", "parallel"), name="log_forget",
    )(x_bf, wf_t, bf_col)


def _cumsum_kernel(x_ref, o_ref):
    x = x_ref[...]
    lane = lax.broadcasted_iota(jnp.int32, x.shape, 1)
    shift = 1
    while shift < x.shape[1]:
        x = x + jnp.where(lane >= shift, pltpu.roll(x, shift, axis=1), 0.0)
        shift *= 2
    o_ref[...] = x


def _cumsum_tokens(logf):
    nb, h, t = logf.shape
    return pl.pallas_call(
        _cumsum_kernel, grid=(nb,),
        in_specs=[pl.BlockSpec((None, h, t), lambda b: (b, 0, 0))],
        out_specs=pl.BlockSpec((None, h, t), lambda b: (b, 0, 0)),
        out_shape=jax.ShapeDtypeStruct(logf.shape, F32),
        compiler_params=_params("parallel"), name="cumsum_tokens",
    )(logf)


def _sb_tile(q, k, v, uu, carry, mask):
    bk = k.shape[0]
    z = lax.dot_general(q, k, _NT, preferred_element_type=F32)
    log1m = -_softplus(z)
    summed = log1m if mask is None else jnp.where(mask, log1m, 0.0)
    hi = summed.astype(BF16)
    lo = (summed - hi.astype(F32)).astype(BF16)
    st = jnp.dot(jnp.concatenate([hi, lo], axis=1), uu, preferred_element_type=F32)
    logw = z + log1m + st[:, :bk]
    if carry is not None:
        logw = logw + carry
    w = jnp.exp(logw)
    if mask is not None:
        w = jnp.where(mask, w, 0.0)
    return jnp.dot(w.astype(BF16), v, preferred_element_type=F32), st[:, bk:]


def _sb_kernel(*refs, blk, has_past):
    if has_past:
        q_ref, kn_ref, vn_ref, kp_ref, vp_ref, o_ref, uu_ref, acc_ref, r_ref = refs
    else:
        q_ref, kn_ref, vn_ref, o_ref, uu_ref, acc_ref, r_ref = refs
        kp_ref, vp_ref = kn_ref, vn_ref
    n_q = q_ref.shape[0] // blk

    key = lax.broadcasted_iota(jnp.int32, uu_ref.shape, 0) % blk
    col = lax.broadcasted_iota(jnp.int32, uu_ref.shape, 1)
    uu_ref[...] = jnp.where((col >= blk) | (key > col), 1.0, 0.0).astype(BF16)

    row = lax.broadcasted_iota(jnp.int32, (blk, blk), 0)
    colq = lax.broadcasted_iota(jnp.int32, (blk, blk), 1)
    strict = colq < row

    def q_block(i, _):
        t0 = pl.multiple_of(i * blk, blk)
        q = q_ref[pl.ds(t0, blk), :]
        pv, total = _sb_tile(q, kn_ref[pl.ds(t0, blk), :].astype(BF16),
                             vn_ref[pl.ds(t0, blk), :].astype(BF16), uu_ref[...], None, strict)
        acc_ref[...] = pv
        r_ref[...] = total
        n_past = kp_ref.shape[0] // blk if has_past else i

        def cond(c):
            return (c[0] >= 0) & (c[1] > EXP_ZERO_BELOW)

        def body(c):
            s0 = pl.multiple_of(c[0] * blk, blk)
            pv, total = _sb_tile(q_ref[pl.ds(t0, blk), :], kp_ref[pl.ds(s0, blk), :].astype(BF16),
                                 vp_ref[pl.ds(s0, blk), :].astype(BF16), uu_ref[...], r_ref[...], None)
            acc_ref[...] += pv
            r = r_ref[...] + total
            r_ref[...] = r
            return c[0] - 1, jnp.max(r)

        lax.while_loop(cond, body, (n_past - 1, jnp.max(total)))
        o_ref[pl.ds(t0, blk), :] = acc_ref[...].astype(o_ref.dtype)
        return 0

    lax.fori_loop(0, n_q, q_block, 0)


def _head_spec(rows, lead):
    return pl.BlockSpec((None,) * (len(lead) + 1) + (rows, HEAD_DIM), lambda b, h: (*lead, b, 0, h))


def _stick_breaking(q, kn, vn, past=None):
    nb, t, w = q.shape
    blk = min(t, LANES)
    in_specs = [_head_spec(t, ())] * 3
    args = [q, kn, vn]
    if past is not None:
        ck, cv, layer = past
        in_specs += [_head_spec(ck.shape[2], (layer,))] * 2
        args += [ck, cv]
    return pl.pallas_call(
        functools.partial(_sb_kernel, blk=blk, has_past=past is not None),
        grid=(nb, w // HEAD_DIM), in_specs=in_specs, out_specs=_head_spec(t, ()),
        out_shape=jax.ShapeDtypeStruct(q.shape, BF16),
        scratch_shapes=[pltpu.VMEM((2 * blk, blk + LANES), BF16),
                        pltpu.VMEM((blk, HEAD_DIM), F32), pltpu.VMEM((blk, LANES), F32)],
        compiler_params=_params("parallel", "parallel"), name="stick_breaking",
    )(*args)


def _fox_kernel(*refs, bq, bk, has_past):
    if has_past:
        q_ref, kn_ref, vn_ref, f_ref, kp_ref, vp_ref, o_ref = refs
        new_off = kp_ref.shape[0]
    else:
        q_ref, kn_ref, vn_ref, f_ref, o_ref = refs
        kp_ref, vp_ref = kn_ref, vn_ref
        new_off = 0
    n_q = q_ref.shape[0] // bq
    row = lax.broadcasted_iota(jnp.int32, (bq, bq), 0)
    col = lax.broadcasted_iota(jnp.int32, (bq, bq), 1)
    causal = col <= row

    def q_block(i, _):
        t0 = pl.multiple_of(i * bq, bq)
        q = q_ref[pl.ds(t0, bq), :]
        fd = f_ref[:, pl.ds(pl.multiple_of(new_off + t0, bq), bq)]
        f0 = fd[:, 0:1]
        z = lax.dot_general(q, kn_ref[pl.ds(t0, bq), :].astype(BF16), _NT, preferred_element_type=F32)
        z = jnp.where(causal, z - (fd - f0), MASK_VALUE)
        m = jnp.max(z, axis=1, keepdims=True)
        p = jnp.exp(z - m)
        l = jnp.sum(p, axis=1, keepdims=True)
        acc = jnp.dot(p.astype(BF16), vn_ref[pl.ds(t0, bq), :].astype(BF16), preferred_element_type=F32)
        n_past = kp_ref.shape[0] // bk if has_past else i * (bq // bk)

        def body(j, carry):
            m, l, acc = carry
            s0 = pl.multiple_of((n_past - 1 - j) * bk, bk)
            z = lax.dot_general(q_ref[pl.ds(t0, bq), :], kp_ref[pl.ds(s0, bk), :].astype(BF16), _NT,
                                preferred_element_type=F32)
            z = z - (f_ref[:, pl.ds(s0, bk)] - f0)
            m_new = jnp.maximum(m, jnp.max(z, axis=1, keepdims=True))
            a = jnp.exp(m - m_new)
            p = jnp.exp(z - m_new)
            l = a * l + jnp.sum(p, axis=1, keepdims=True)
            acc = a * acc + jnp.dot(p.astype(BF16), vp_ref[pl.ds(s0, bk), :].astype(BF16),
                                    preferred_element_type=F32)
            return m_new, l, acc

        m, l, acc = lax.fori_loop(0, n_past, body, (m, l, acc))
        o_ref[pl.ds(t0, bq), :] = (acc / l).astype(o_ref.dtype)
        return 0

    lax.fori_loop(0, n_q, q_block, 0)


def _forgetting(q, kn, vn, f_cum, past=None):
    nb, t, w = q.shape
    bq = min(t, 2 * LANES)
    in_specs = [_head_spec(t, ())] * 3
    in_specs.append(pl.BlockSpec((None, None, 1, f_cum.shape[-1]), lambda b, h: (b, h, 0, 0)))
    args = [q, kn, vn, f_cum]
    if past is not None:
        ck, cv, layer = past
        in_specs += [_head_spec(ck.shape[2], (layer,))] * 2
        args += [ck, cv]
    return pl.pallas_call(
        functools.partial(_fox_kernel, bq=bq, bk=bq, has_past=past is not None),
        grid=(nb, w // HEAD_DIM), in_specs=in_specs, out_specs=_head_spec(t, ()),
        out_shape=jax.ShapeDtypeStruct(q.shape, BF16),
        compiler_params=_params("parallel", "parallel"), name="forgetting",
    )(*args)


def _mem_kernel(q_ref, mk_ref, mv_ref, o_ref):
    for h in range(q_ref.shape[1] // HEAD_DIM):
        sl = slice(h * HEAD_DIM, (h + 1) * HEAD_DIM)
        z = lax.dot_general(q_ref[:, sl], mk_ref[:, sl].astype(BF16), _NT, preferred_element_type=F32)
        p = jnp.exp(z - jnp.max(z, axis=1, keepdims=True))
        o = jnp.dot(p.astype(BF16), mv_ref[:, sl].astype(BF16), preferred_element_type=F32)
        o_ref[:, sl] = (o / jnp.sum(p, axis=1, keepdims=True)).astype(o_ref.dtype)


def _memory_attention(q, mk, mv, lead=()):
    nb, t, w = q.shape
    tq = _tile(t, 512)
    mem_spec = pl.BlockSpec((None,) * (len(lead) + 1) + mk.shape[-2:], lambda b, j: (*lead, b, 0, 0))
    return pl.pallas_call(
        _mem_kernel, grid=(nb, t // tq),
        in_specs=[pl.BlockSpec((None, tq, w), lambda b, j: (b, j, 0)), mem_spec, mem_spec],
        out_specs=pl.BlockSpec((None, tq, w), lambda b, j: (b, j, 0)),
        out_shape=jax.ShapeDtypeStruct(q.shape, BF16),
        compiler_params=_params("parallel", "parallel"), name="memory_attention",
    )(q, mk, mv)


def _merge_kernel(x_ref, o_sb, o_fx, o_mem, z_sb, z_fx, z_mem, g_sb, g_fx, g_mem,
                  w_sb, w_fx, w_mem, w_out, ln_g, ln_b, y_ref, ybf_ref, *, alpha):
    def branch(o_ref, z_ref, g_ref, w_ref):
        y = jnp.dot(o_ref[...] * z_ref[...], w_ref[...], preferred_element_type=F32)
        return g_ref[...].astype(F32) * y

    merged = branch(o_sb, z_sb, g_sb, w_sb) + branch(o_fx, z_fx, g_fx, w_fx) + branch(o_mem, z_mem, g_mem, w_mem)
    r = alpha * x_ref[...] + jnp.dot(merged.astype(BF16), w_out[...], preferred_element_type=F32)
    c = r - jnp.mean(r, axis=-1, keepdims=True)
    var = jnp.mean(c * c, axis=-1, keepdims=True)
    y = c * lax.rsqrt(var + LN_EPS) * ln_g[...] + ln_b[...]
    y_ref[...] = y
    ybf_ref[...] = y.astype(BF16)


def _merge(x, acts, weights, ln_g, ln_b, alpha):
    n, d = x.shape
    tm = _tile(n, 512)
    tok = lambda a: pl.BlockSpec((tm, a.shape[1]), lambda i: (i, 0))
    full = lambda a: pl.BlockSpec(a.shape, lambda i: (0, 0))
    return pl.pallas_call(
        functools.partial(_merge_kernel, alpha=alpha), grid=(n // tm,),
        in_specs=[tok(x)] + [tok(a) for a in acts] + [full(w) for w in weights] + [full(ln_g), full(ln_b)],
        out_specs=[tok(x), tok(x)],
        out_shape=[jax.ShapeDtypeStruct((n, d), F32), jax.ShapeDtypeStruct((n, d), BF16)],
        compiler_params=_params("parallel"), name="merge",
    )(x, *acts, *weights, ln_g, ln_b)


def _pad_rows(a, rows):
    return jnp.pad(a, ((0, 0), (0, rows - a.shape[1]), (0, 0)))


def kernel(x_prompt, x_sample, cache_sb_k, cache_sb_v, cache_fox_k, cache_fox_v, cache_fox_logf,
           cache_mem_k, cache_mem_v, mem_prompt, w_in, b_f, w_mem_kv, w_br_sb, w_br_fox, w_br_mem,
           w_out, ln_g, ln_b):
    depth, d, _ = w_in.shape
    nb, t, _ = x_prompt.shape
    nbs, ts, _ = x_sample.shape
    past = cache_sb_k.shape[2]
    h_sb, h_fx, h_mem = cache_sb_k.shape[3], cache_fox_k.shape[3], cache_mem_k.shape[3]
    w_sb, w_fx, w_mem = h_sb * HEAD_DIM, h_fx * HEAD_DIM, h_mem * HEAD_DIM
    n_mem = mem_prompt.shape[1]
    alpha = float((2 * depth) ** 0.25)
    scale = HEAD_DIM ** -0.5
    ts_pad = -(-ts // LANES) * LANES

    sizes = [w_sb] * 4 + [w_fx] * 4 + [h_fx] + [w_mem] * 2 + [d] * 3
    offs = [0]
    for s in sizes:
        offs.append(offs[-1] + s)
    (c_qsb, c_ksb, c_vsb, c_zsb, c_qfx, c_kfx, c_vfx, c_zfx, c_f, c_qm, c_zm, c_gsb, c_gfx, c_gm) = [
        (offs[i], offs[i + 1]) for i in range(len(sizes))]

    w_in_bf = w_in.astype(BF16)
    w_mem_kv_bf = w_mem_kv.astype(BF16)
    w_br_bf = [w.astype(BF16) for w in (w_br_sb, w_br_fox, w_br_mem, w_out)]
    cache_flat = [c.reshape(*c.shape[:3], -1) for c in (cache_sb_k, cache_sb_v, cache_fox_k, cache_fox_v,
                                                        cache_mem_k, cache_mem_v)]
    c_sb_k, c_sb_v, c_fx_k, c_fx_v, c_mk, c_mv = cache_flat
    c_logf = jnp.pad(jnp.transpose(cache_fox_logf.astype(F32), (0, 1, 3, 2)),
                     ((0, 0), (0, 0), (0, SUBLANES - h_fx), (0, 0)))
    mem_bf = mem_prompt.reshape(nb * n_mem, d).astype(BF16)

    xp, xs = x_prompt.reshape(nb * t, d), x_sample.reshape(nbs * ts, d)
    xp_bf, xs_bf = xp.astype(BF16), xs.astype(BF16)
    p_state = [[] for _ in range(7)]
    s_state = [[] for _ in range(5)]

    for l in range(depth):
        wl = lambda c: w_in_bf[l, :, c[0]:c[1]]
        w_kv = [wl(c) for c in (c_ksb, c_vsb, c_kfx, c_vfx)]
        w_qz = [wl(c) for c in (c_qsb, c_qfx, c_qm, c_zsb, c_zfx, c_zm)]
        w_g = [wl(c) for c in (c_gsb, c_gfx, c_gm)]
        wf_t = jnp.pad(jnp.transpose(wl(c_f)), ((0, SUBLANES - h_fx), (0, 0)))
        bf_col = jnp.pad(b_f[l].astype(F32), (0, SUBLANES - h_fx)).reshape(SUBLANES, 1)
        merge_w = [w[l] for w in w_br_bf]
        g_row, b_row = ln_g[l].astype(F32).reshape(1, d), ln_b[l].astype(F32).reshape(1, d)

        def project(x_bf):
            kv = _project(x_bf, w_kv, ["kv"] * 4, scale)
            qz = _project(x_bf, w_qz, ["q"] * 3 + ["silu"] * 3, scale)
            g = _project(x_bf, w_g, ["sigmoid"] * 3, scale)
            return kv, qz, g

        (k_sb, k_sb_bf, v_sb, v_sb_bf, k_fx, k_fx_bf, v_fx, v_fx_bf), qz, gates = project(xp_bf)
        q_sb, q_fx, q_m, z_sb, z_fx, z_m = qz
        logf = _log_forget(xp_bf, wf_t, bf_col, nb)
        f_cum = _cumsum_tokens(logf)[:, :h_fx, None, :]
        mk, mk_bf, mv, mv_bf = _project(mem_bf, [w_mem_kv_bf[l, :, :w_mem], w_mem_kv_bf[l, :, w_mem:]],
                                        ["kv"] * 2, scale)
        r3 = lambda a: a.reshape(nb, t, -1)
        o_sb = _stick_breaking(r3(q_sb), r3(k_sb_bf), r3(v_sb_bf))
        o_fx = _forgetting(r3(q_fx), r3(k_fx_bf), r3(v_fx_bf), f_cum)
        o_m = _memory_attention(r3(q_m), mk_bf.reshape(nb, n_mem, w_mem), mv_bf.reshape(nb, n_mem, w_mem))
        f2 = lambda a: a.reshape(nb * t, -1)
        xp, xp_bf = _merge(xp, [f2(o_sb), f2(o_fx), f2(o_m), z_sb, z_fx, z_m, *gates], merge_w, g_row, b_row, alpha)
        for lst, arr in zip(p_state, (k_sb.reshape(nb, t, h_sb, HEAD_DIM), v_sb.reshape(nb, t, h_sb, HEAD_DIM),
                                      k_fx.reshape(nb, t, h_fx, HEAD_DIM), v_fx.reshape(nb, t, h_fx, HEAD_DIM),
                                      jnp.transpose(logf[:, :h_fx, :], (0, 2, 1)),
                                      mk.reshape(nb, n_mem, h_mem, HEAD_DIM), mv.reshape(nb, n_mem, h_mem, HEAD_DIM))):
            lst.append(arr)

        (k_sb, k_sb_bf, v_sb, v_sb_bf, k_fx, k_fx_bf, v_fx, v_fx_bf), qz, gates = project(xs_bf)
        q_sb, q_fx, q_m, z_sb, z_fx, z_m = qz
        logf = _log_forget(xs_bf, wf_t, bf_col, 1)
        logf = jnp.transpose(logf.reshape(SUBLANES, nbs, ts), (1, 0, 2))
        f_all = jnp.concatenate([c_logf[l], jnp.pad(logf, ((0, 0), (0, 0), (0, ts_pad - ts)))], axis=-1)
        f_cum = _cumsum_tokens(f_all)[:, :h_fx, None, :]
        r3 = lambda a: _pad_rows(a.reshape(nbs, ts, -1), ts_pad)
        o_sb = _stick_breaking(r3(q_sb), r3(k_sb_bf), r3(v_sb_bf), past=(c_sb_k, c_sb_v, l))
        o_fx = _forgetting(r3(q_fx), r3(k_fx_bf), r3(v_fx_bf), f_cum, past=(c_fx_k, c_fx_v, l))
        o_m = _memory_attention(q_m.reshape(nbs, ts, -1), c_mk, c_mv, lead=(l,))
        f2 = lambda a: a[:, :ts].reshape(nbs * ts, -1)
        xs, xs_bf = _merge(xs, [f2(o_sb), f2(o_fx), f2(o_m), z_sb, z_fx, z_m, *gates], merge_w, g_row, b_row, alpha)
        for lst, arr in zip(s_state, (k_sb.reshape(nbs, ts, h_sb, HEAD_DIM), v_sb.reshape(nbs, ts, h_sb, HEAD_DIM),
                                      k_fx.reshape(nbs, ts, h_fx, HEAD_DIM), v_fx.reshape(nbs, ts, h_fx, HEAD_DIM),
                                      jnp.transpose(logf[:, :h_fx, :], (0, 2, 1)))):
            lst.append(arr)

    p_out = [jnp.stack(a, axis=0) for a in p_state]
    s_out = [jnp.stack(a, axis=0) for a in s_state]
    return (xp.reshape(nb, t, d), xs.reshape(nbs, ts, d), *p_out, *s_out)
```

```python
import functools

import jax
import jax.numpy as jnp
from jax import lax
from jax.experimental import pallas as pl
from jax.experimental.pallas import tpu as pltpu

F32 = jnp.float32
BF16 = jnp.bfloat16
HEAD_DIM = 128
LANES = 128
SUBLANES = 8
LN_EPS = 1e-5
MASK_VALUE = -1e30
EXP_ZERO_BELOW = -104.0
VMEM_LIMIT_BYTES = 56 * 1024 * 1024
TOKEN_TILE = 512
Q_SUPER = 1024
SB_BLOCK = 128
FOX_BLOCK = 512

_NT = (((1,), (1,)), ((), ()))


def _params(*sem):
    return pltpu.CompilerParams(dimension_semantics=sem, vmem_limit_bytes=VMEM_LIMIT_BYTES)


def _tile(n, target):
    for t in range(min(n, target), 0, -1):
        if n % t == 0 and t % SUBLANES == 0:
            return t
    return n


def _sigmoid(x):
    return 1.0 / (1.0 + jnp.exp(-x))


def _softplus(x):
    return jnp.maximum(x, 0.0) + jnp.log(1.0 + jnp.exp(-jnp.abs(x)))


def _proj_kernel(x_ref, *refs, kinds, scale, head_major, n_alias):
    n_seg = len(kinds)
    w_refs = refs[:n_seg]
    refs = refs[n_seg:]
    if "logf" in kinds:
        bf_ref, refs = refs[0], refs[1:]
    out_refs = refs[n_alias:]
    x = x_ref[...]
    oi = 0

    def put(o_ref, val):
        if head_major:
            for h in range(o_ref.shape[0]):
                o_ref[h] = val[:, h * HEAD_DIM:(h + 1) * HEAD_DIM].astype(o_ref.dtype)
        else:
            o_ref[...] = val.astype(o_ref.dtype)

    for w_ref, kind in zip(w_refs, kinds):
        acc = jnp.dot(x, w_ref[...], preferred_element_type=F32)
        if kind == "kv":
            put(out_refs[oi], acc)
            put(out_refs[oi + 1], acc)
            oi += 2
        elif kind == "q":
            put(out_refs[oi], acc * scale)
            oi += 1
        elif kind == "q_tok":
            out_refs[oi][...] = (acc * scale).astype(BF16)
            oi += 1
        elif kind == "logf":
            lf = -_softplus(-(acc + bf_ref[...]))
            out_refs[oi][...] = lf.T[:SUBLANES, :] if head_major else lf
            oi += 1
        else:
            out_refs[oi][...] = (acc * _sigmoid(acc) if kind == "silu" else _sigmoid(acc)).astype(BF16)
            oi += 1


def _project(x_bf, nb, weights, kinds, scale, *, head_major, bias=None, state=None, layer=0):
    n, d = x_bf.shape
    t = n // nb
    tm = _tile(t, TOKEN_TILE)
    tpb = t // tm
    const = lambda a: pl.BlockSpec(a.shape, lambda b, j: (0,) * a.ndim)
    tok = lambda w: pl.BlockSpec((tm, w), lambda b, j: (b * tpb + j, 0))
    in_specs = [tok(d)] + [const(w) for w in weights]
    args = [x_bf, *weights]
    if bias is not None:
        in_specs.append(const(bias))
        args.append(bias)
    out_shape, out_specs, aliases = [], [], {}
    for w, kind in zip(weights, kinds):
        width = w.shape[1]
        heads = width // HEAD_DIM
        if kind in ("kv", "q") and head_major:
            if kind == "kv":
                depth = state[0]
                out_shape.append(jax.ShapeDtypeStruct((depth, nb, heads, t, HEAD_DIM), F32))
                out_specs.append(pl.BlockSpec((None, None, heads, tm, HEAD_DIM),
                                              lambda b, j: (layer, b, 0, j, 0)))
            out_shape.append(jax.ShapeDtypeStruct((nb, heads, t, HEAD_DIM), BF16))
            out_specs.append(pl.BlockSpec((None, heads, tm, HEAD_DIM), lambda b, j: (b, 0, j, 0)))
        elif kind == "logf" and head_major:
            out_shape.append(jax.ShapeDtypeStruct((nb, SUBLANES, t), F32))
            out_specs.append(pl.BlockSpec((None, SUBLANES, tm), lambda b, j: (b, 0, j)))
        else:
            for dt in {"kv": (F32, BF16), "logf": (F32,)}.get(kind, (BF16,)):
                out_shape.append(jax.ShapeDtypeStruct((n, width), dt))
                out_specs.append(tok(width))
    bufs = [] if state is None or state[1] is None else list(state[1])
    if bufs:
        f32_outs = [i for i, s in enumerate(out_shape) if s.dtype == F32]
        for k, buf in enumerate(bufs):
            aliases[len(args)] = f32_outs[k]
            in_specs.append(pl.BlockSpec(memory_space=pl.ANY))
            args.append(buf)
    return pl.pallas_call(
        functools.partial(_proj_kernel, kinds=tuple(kinds), scale=scale, head_major=head_major,
                          n_alias=len(bufs)),
        grid=(nb, tpb), in_specs=in_specs, out_specs=out_specs, out_shape=out_shape,
        input_output_aliases=aliases,
        compiler_params=_params("parallel", "parallel"), name="project",
    )(*args)


def _cumsum_kernel(x_ref, o_ref):
    x = x_ref[...]
    lane = lax.broadcasted_iota(jnp.int32, x.shape, 1)
    shift = 1
    while shift < x.shape[1]:
        x = x + jnp.where(lane >= shift, pltpu.roll(x, shift, axis=1), 0.0)
        shift *= 2
    o_ref[...] = x


def _cumsum_tokens(logf):
    nb, h, t = logf.shape
    return pl.pallas_call(
        _cumsum_kernel, grid=(nb,),
        in_specs=[pl.BlockSpec((None, h, t), lambda b: (b, 0, 0))],
        out_specs=pl.BlockSpec((None, h, t), lambda b: (b, 0, 0)),
        out_shape=jax.ShapeDtypeStruct(logf.shape, F32),
        compiler_params=_params("parallel"), name="cumsum_tokens",
    )(logf)


def _attention_call(body, name, q, kn, vn, hb, extra, past, scratch):
    nb, nh, tq, _ = q.shape
    tqs = min(tq, Q_SUPER)
    in_specs = [pl.BlockSpec((None, hb, tqs, HEAD_DIM), lambda b, g, s: (b, g, s, 0))]
    in_specs += [pl.BlockSpec((None, hb, kn.shape[2], HEAD_DIM), lambda b, g, s: (b, g, 0, 0))] * 2
    args = [q, kn, vn]
    for a in extra:
        in_specs.append(pl.BlockSpec((None, hb, 1, a.shape[-1]), lambda b, g, s: (b, g, 0, 0)))
        args.append(a)
    if past is not None:
        ck, cv, layer = past
        in_specs += [pl.BlockSpec((None, None, hb, ck.shape[3], HEAD_DIM),
                                  lambda b, g, s: (layer, b, g, 0, 0))] * 2
        args += [ck, cv]
    return pl.pallas_call(
        body, grid=(nb, nh // hb, tq // tqs), in_specs=in_specs,
        out_specs=pl.BlockSpec((None, tqs, hb * HEAD_DIM), lambda b, g, s: (b, s, g)),
        out_shape=jax.ShapeDtypeStruct((nb, tq, nh * HEAD_DIM), BF16),
        scratch_shapes=scratch,
        compiler_params=_params("parallel", "parallel", "arbitrary"), name=name,
    )(*args)


def _sb_tile(q, k, v, uu, carry, mask):
    bk = k.shape[0]
    z = lax.dot_general(q, k, _NT, preferred_element_type=F32)
    log1m = -_softplus(z)
    summed = log1m if mask is None else jnp.where(mask, log1m, 0.0)
    hi = summed.astype(BF16)
    lo = (summed - hi.astype(F32)).astype(BF16)
    st = jnp.dot(jnp.concatenate([hi, lo], axis=1), uu, preferred_element_type=F32)
    logw = z + log1m + st[:, :bk]
    if carry is not None:
        logw = logw + carry
    w = jnp.exp(logw)
    if mask is not None:
        w = jnp.where(mask, w, 0.0)
    return jnp.dot(w.astype(BF16), v, preferred_element_type=F32), st[:, bk:]


def _sb_kernel(*refs, bq, blk, has_past):
    if has_past:
        q_ref, kn_ref, vn_ref, kp_ref, vp_ref, o_ref, uu_ref, acc_ref, r_ref = refs
    else:
        q_ref, kn_ref, vn_ref, o_ref, uu_ref, acc_ref, r_ref = refs
        kp_ref, vp_ref = kn_ref, vn_ref
    hb, tqs, _ = q_ref.shape
    base = pl.program_id(2) * tqs

    key = lax.broadcasted_iota(jnp.int32, uu_ref.shape, 0) % blk
    col = lax.broadcasted_iota(jnp.int32, uu_ref.shape, 1)
    uu_ref[...] = jnp.where((col >= blk) | (key > col), 1.0, 0.0).astype(BF16)
    strict = (lax.broadcasted_iota(jnp.int32, (bq, blk), 1) < lax.broadcasted_iota(jnp.int32, (bq, blk), 0))

    def q_block(i, _):
        r0 = pl.multiple_of(i * bq, bq)
        d0 = 0 if has_past else pl.multiple_of(base + r0, blk)
        top = None
        for h in range(hb):
            pv, total = _sb_tile(q_ref[h, pl.ds(r0, bq), :], kn_ref[h, pl.ds(d0, blk), :].astype(BF16),
                                 vn_ref[h, pl.ds(d0, blk), :].astype(BF16), uu_ref[...], None, strict)
            acc_ref[h] = pv
            r_ref[h] = total
            top = total if top is None else jnp.maximum(top, total)
        n_past = kp_ref.shape[1] // blk if has_past else (base + r0) // blk

        def cond(c):
            return (c[0] >= 0) & (c[1] > EXP_ZERO_BELOW)

        def body(c):
            s0 = pl.multiple_of(c[0] * blk, blk)
            top = None
            for h in range(hb):
                pv, total = _sb_tile(q_ref[h, pl.ds(r0, bq), :], kp_ref[h, pl.ds(s0, blk), :].astype(BF16),
                                     vp_ref[h, pl.ds(s0, blk), :].astype(BF16), uu_ref[...], r_ref[h], None)
                acc_ref[h] += pv
                r = r_ref[h] + total
                r_ref[h] = r
                top = r if top is None else jnp.maximum(top, r)
            return c[0] - 1, jnp.max(top)

        lax.while_loop(cond, body, (n_past - 1, jnp.max(top)))
        for h in range(hb):
            o_ref[pl.ds(r0, bq), h * HEAD_DIM:(h + 1) * HEAD_DIM] = acc_ref[h].astype(o_ref.dtype)
        return 0

    lax.fori_loop(0, tqs // bq, q_block, 0)


def _stick_breaking(q, kn, vn, hb, past=None):
    bq = min(q.shape[2], SB_BLOCK)
    scratch = [pltpu.VMEM((2 * SB_BLOCK, SB_BLOCK + LANES), BF16),
               pltpu.VMEM((hb, bq, HEAD_DIM), F32), pltpu.VMEM((hb, bq, LANES), F32)]
    body = functools.partial(_sb_kernel, bq=bq, blk=SB_BLOCK, has_past=past is not None)
    return _attention_call(body, "stick_breaking", q, kn, vn, hb, (), past, scratch)


def _fox_kernel(*refs, bq, bk, has_past):
    if has_past:
        q_ref, kn_ref, vn_ref, f_ref, kp_ref, vp_ref, o_ref = refs
        new_off = kp_ref.shape[1]
    else:
        q_ref, kn_ref, vn_ref, f_ref, o_ref = refs
        kp_ref, vp_ref = kn_ref, vn_ref
        new_off = 0
    hb, tqs, _ = q_ref.shape
    bd = min(kn_ref.shape[1], bk)
    base = pl.program_id(2) * tqs
    causal = (lax.broadcasted_iota(jnp.int32, (bq, bd), 1) <= lax.broadcasted_iota(jnp.int32, (bq, bd), 0))

    def q_block(i, _):
        r0 = pl.multiple_of(i * bq, bq)
        d0 = 0 if has_past else pl.multiple_of(base + r0, bd)
        n_past = kp_ref.shape[1] // bk if has_past else (base + r0) // bk
        state, f0s = [], []
        for h in range(hb):
            fd = f_ref[h, :, pl.ds(new_off + d0, bd)]
            f0 = fd[:, 0:1]
            z = lax.dot_general(q_ref[h, pl.ds(r0, bq), :], kn_ref[h, pl.ds(d0, bd), :].astype(BF16), _NT,
                                preferred_element_type=F32)
            z = jnp.where(causal, z - (fd - f0), MASK_VALUE)
            m = jnp.max(z, axis=1, keepdims=True)
            p = jnp.exp(z - m)
            l = jnp.sum(p, axis=1, keepdims=True)
            acc = jnp.dot(p.astype(BF16), vn_ref[h, pl.ds(d0, bd), :].astype(BF16), preferred_element_type=F32)
            state += [m, l, acc]
            f0s.append(f0)

        def body(j, carry):
            s0 = pl.multiple_of((n_past - 1 - j) * bk, bk)
            out = []
            for h in range(hb):
                m, l, acc = carry[3 * h:3 * h + 3]
                z = lax.dot_general(q_ref[h, pl.ds(r0, bq), :], kp_ref[h, pl.ds(s0, bk), :].astype(BF16), _NT,
                                    preferred_element_type=F32)
                z = z - (f_ref[h, :, pl.ds(s0, bk)] - f0s[h])
                m_new = jnp.maximum(m, jnp.max(z, axis=1, keepdims=True))
                a = jnp.exp(m - m_new)
                p = jnp.exp(z - m_new)
                l = a * l + jnp.sum(p, axis=1, keepdims=True)
                acc = a * acc + jnp.dot(p.astype(BF16), vp_ref[h, pl.ds(s0, bk), :].astype(BF16),
                                        preferred_element_type=F32)
                out += [m_new, l, acc]
            return tuple(out)

        state = lax.fori_loop(0, n_past, body, tuple(state))
        for h in range(hb):
            o_ref[pl.ds(r0, bq), h * HEAD_DIM:(h + 1) * HEAD_DIM] = (state[3 * h + 2] / state[3 * h + 1]).astype(o_ref.dtype)
        return 0

    lax.fori_loop(0, tqs // bq, q_block, 0)


def _forgetting(q, kn, vn, f_cum, hb, past=None):
    bq = min(q.shape[2], FOX_BLOCK)
    body = functools.partial(_fox_kernel, bq=bq, bk=FOX_BLOCK, has_past=past is not None)
    return _attention_call(body, "forgetting", q, kn, vn, hb, (f_cum,), past, [])


def _mem_kernel(q_ref, mk_ref, mv_ref, o_ref):
    for h in range(q_ref.shape[1] // HEAD_DIM):
        sl = slice(h * HEAD_DIM, (h + 1) * HEAD_DIM)
        z = lax.dot_general(q_ref[:, sl], mk_ref[:, sl].astype(BF16), _NT, preferred_element_type=F32)
        p = jnp.exp(z - jnp.max(z, axis=1, keepdims=True))
        o = jnp.dot(p.astype(BF16), mv_ref[:, sl].astype(BF16), preferred_element_type=F32)
        o_ref[:, sl] = (o / jnp.sum(p, axis=1, keepdims=True)).astype(o_ref.dtype)


def _memory_attention(q, mk, mv, lead=()):
    nb, t, w = q.shape
    tq = _tile(t, TOKEN_TILE)
    mem_spec = pl.BlockSpec((None,) * (len(lead) + 1) + mk.shape[-2:], lambda b, j: (*lead, b, 0, 0))
    return pl.pallas_call(
        _mem_kernel, grid=(nb, t // tq),
        in_specs=[pl.BlockSpec((None, tq, w), lambda b, j: (b, j, 0)), mem_spec, mem_spec],
        out_specs=pl.BlockSpec((None, tq, w), lambda b, j: (b, j, 0)),
        out_shape=jax.ShapeDtypeStruct(q.shape, BF16),
        compiler_params=_params("parallel", "parallel"), name="memory_attention",
    )(q, mk, mv)


def _merge_kernel(x_ref, o_sb, o_fx, o_mem, z_sb, z_fx, z_mem, g_sb, g_fx, g_mem,
                  w_sb, w_fx, w_mem, w_out, ln_g, ln_b, y_ref, ybf_ref, *, alpha):
    def branch(o_ref, z_ref, g_ref, w_ref):
        y = jnp.dot(o_ref[...] * z_ref[...], w_ref[...], preferred_element_type=F32)
        return g_ref[...].astype(F32) * y

    merged = branch(o_sb, z_sb, g_sb, w_sb) + branch(o_fx, z_fx, g_fx, w_fx) + branch(o_mem, z_mem, g_mem, w_mem)
    r = alpha * x_ref[...] + jnp.dot(merged.astype(BF16), w_out[...], preferred_element_type=F32)
    c = r - jnp.mean(r, axis=-1, keepdims=True)
    var = jnp.mean(c * c, axis=-1, keepdims=True)
    y = c * lax.rsqrt(var + LN_EPS) * ln_g[...] + ln_b[...]
    y_ref[...] = y
    ybf_ref[...] = y.astype(BF16)


def _merge(x, acts, weights, ln_g, ln_b, alpha):
    n, d = x.shape
    tm = _tile(n, TOKEN_TILE)
    tok = lambda a: pl.BlockSpec((tm, a.shape[1]), lambda i: (i, 0))
    full = lambda a: pl.BlockSpec(a.shape, lambda i: (0, 0))
    return pl.pallas_call(
        functools.partial(_merge_kernel, alpha=alpha), grid=(n // tm,),
        in_specs=[tok(x)] + [tok(a) for a in acts] + [full(w) for w in weights] + [full(ln_g), full(ln_b)],
        out_specs=[tok(x), tok(x)],
        out_shape=[jax.ShapeDtypeStruct((n, d), F32), jax.ShapeDtypeStruct((n, d), BF16)],
        compiler_params=_params("parallel"), name="merge",
    )(x, *acts, *weights, ln_g, ln_b)


def _head_group(n_heads, target):
    return max(g for g in range(1, min(n_heads, target) + 1) if n_heads % g == 0)


def kernel(x_prompt, x_sample, cache_sb_k, cache_sb_v, cache_fox_k, cache_fox_v, cache_fox_logf,
           cache_mem_k, cache_mem_v, mem_prompt, w_in, b_f, w_mem_kv, w_br_sb, w_br_fox, w_br_mem,
           w_out, ln_g, ln_b):
    depth, d, _ = w_in.shape
    nb, t, _ = x_prompt.shape
    nbs, ts, _ = x_sample.shape
    h_sb, h_fx, h_mem = cache_sb_k.shape[3], cache_fox_k.shape[3], cache_mem_k.shape[3]
    w_sb, w_fx, w_mem = h_sb * HEAD_DIM, h_fx * HEAD_DIM, h_mem * HEAD_DIM
    n_mem = mem_prompt.shape[1]
    alpha = float((2 * depth) ** 0.25)
    scale = HEAD_DIM ** -0.5
    ts_pad = -(-ts // LANES) * LANES

    sizes = [w_sb] * 4 + [w_fx] * 4 + [h_fx] + [w_mem] * 2 + [d] * 3
    offs = [0]
    for s in sizes:
        offs.append(offs[-1] + s)
    (c_qsb, c_ksb, c_vsb, c_zsb, c_qfx, c_kfx, c_vfx, c_zfx, c_f, c_qm, c_zm, c_gsb, c_gfx, c_gm) = [
        (offs[i], offs[i + 1]) for i in range(len(sizes))]

    w_in_bf = w_in.astype(BF16)
    w_mem_kv_bf = w_mem_kv.astype(BF16)
    w_br_bf = [w.astype(BF16) for w in (w_br_sb, w_br_fox, w_br_mem, w_out)]
    c_sb_k, c_sb_v, c_fx_k, c_fx_v = [jnp.transpose(c, (0, 1, 3, 2, 4))
                                      for c in (cache_sb_k, cache_sb_v, cache_fox_k, cache_fox_v)]
    c_mk, c_mv = [c.reshape(*c.shape[:3], -1) for c in (cache_mem_k, cache_mem_v)]
    c_logf = jnp.transpose(cache_fox_logf.astype(F32), (0, 1, 3, 2))
    mem_bf = mem_prompt.reshape(nb * n_mem, d).astype(BF16)

    xp, xs = x_prompt.reshape(nb * t, d), x_sample.reshape(nbs * ts, d)
    xp_bf, xs_bf = xp.astype(BF16), xs.astype(BF16)
    p_state = None
    p_rest = [[] for _ in range(3)]
    s_state = [[] for _ in range(5)]
    hb_sb_p, hb_fx_p = _head_group(h_sb, 6), _head_group(h_fx, 3)
    hb_sb_s, hb_fx_s = _head_group(h_sb, 3), _head_group(h_fx, 3)

    for l in range(depth):
        wl = lambda c: w_in_bf[l, :, c[0]:c[1]]
        w_kv = [wl(c) for c in (c_ksb, c_vsb, c_kfx, c_vfx)]
        w_qz = [wl(c) for c in (c_qsb, c_qfx, c_qm, c_zsb, c_zfx, c_zm)]
        w_qz.append(jnp.pad(wl(c_f), ((0, 0), (0, LANES - h_fx))))
        w_g = [wl(c) for c in (c_gsb, c_gfx, c_gm)]
        bf_row = jnp.pad(b_f[l].astype(F32), (0, LANES - h_fx)).reshape(1, LANES)
        merge_w = [w[l] for w in w_br_bf]
        g_row, b_row = ln_g[l].astype(F32).reshape(1, d), ln_b[l].astype(F32).reshape(1, d)

        kv = _project(xp_bf, nb, w_kv, ["kv"] * 4, scale, head_major=True, state=(depth, p_state), layer=l)
        p_state = kv[0::2]
        k_sb_bf, v_sb_bf, k_fx_bf, v_fx_bf = kv[1::2]
        q_sb, q_fx, q_m, z_sb, z_fx, z_m, logf = _project_qz(xp_bf, nb, w_qz, scale, bf_row, True)
        gates = _project(xp_bf, nb, w_g, ["sigmoid"] * 3, scale, head_major=True)
        f_cum = _cumsum_tokens(logf)[:, :h_fx, None, :]
        mk, mk_bf, mv, mv_bf = _project(mem_bf, nb, [w_mem_kv_bf[l, :, :w_mem], w_mem_kv_bf[l, :, w_mem:]],
                                        ["kv"] * 2, scale, head_major=False)
        o_sb = _stick_breaking(q_sb, k_sb_bf, v_sb_bf, hb_sb_p)
        o_fx = _forgetting(q_fx, k_fx_bf, v_fx_bf, f_cum, hb_fx_p)
        o_m = _memory_attention(q_m.reshape(nb, t, w_mem), mk_bf.reshape(nb, n_mem, w_mem),
                                mv_bf.reshape(nb, n_mem, w_mem))
        f2 = lambda a: a.reshape(nb * t, -1)
        xp, xp_bf = _merge(xp, [f2(o_sb), f2(o_fx), f2(o_m), z_sb, z_fx, z_m, *gates], merge_w, g_row, b_row, alpha)
        for lst, arr in zip(p_rest, (jnp.transpose(logf[:, :h_fx, :], (0, 2, 1)),
                                     mk.reshape(nb, n_mem, h_mem, HEAD_DIM), mv.reshape(nb, n_mem, h_mem, HEAD_DIM))):
            lst.append(arr)

        k_sb, k_sb_bf, v_sb, v_sb_bf, k_fx, k_fx_bf, v_fx, v_fx_bf = _project(
            xs_bf, 1, w_kv, ["kv"] * 4, scale, head_major=False)
        q_sb, q_fx, q_m, z_sb, z_fx, z_m, logf = _project_qz(xs_bf, 1, w_qz, scale, bf_row, False)
        gates = _project(xs_bf, 1, w_g, ["sigmoid"] * 3, scale, head_major=False)
        logf = jnp.transpose(logf[:, :h_fx].reshape(nbs, ts, h_fx), (0, 2, 1))
        f_all = jnp.concatenate([c_logf[l], jnp.pad(logf, ((0, 0), (0, 0), (0, ts_pad - ts)))], axis=-1)
        f_all = jnp.pad(f_all, ((0, 0), (0, SUBLANES - h_fx), (0, 0)))
        f_cum = _cumsum_tokens(f_all)[:, :h_fx, None, :]
        hm = lambda a: jnp.transpose(a.reshape(nbs, ts, -1, HEAD_DIM), (0, 2, 1, 3))
        hm_pad = lambda a: jnp.pad(hm(a), ((0, 0), (0, 0), (0, ts_pad - ts), (0, 0)))
        o_sb = _stick_breaking(hm(q_sb), hm_pad(k_sb_bf), hm_pad(v_sb_bf), hb_sb_s, past=(c_sb_k, c_sb_v, l))
        o_fx = _forgetting(hm(q_fx), hm_pad(k_fx_bf), hm_pad(v_fx_bf), f_cum, hb_fx_s, past=(c_fx_k, c_fx_v, l))
        o_m = _memory_attention(q_m.reshape(nbs, ts, w_mem), c_mk, c_mv, lead=(l,))
        f2 = lambda a: a.reshape(nbs * ts, -1)
        xs, xs_bf = _merge(xs, [f2(o_sb), f2(o_fx), f2(o_m), z_sb, z_fx, z_m, *gates], merge_w, g_row, b_row, alpha)
        for lst, arr in zip(s_state, (k_sb.reshape(nbs, ts, h_sb, HEAD_DIM), v_sb.reshape(nbs, ts, h_sb, HEAD_DIM),
                                      k_fx.reshape(nbs, ts, h_fx, HEAD_DIM), v_fx.reshape(nbs, ts, h_fx, HEAD_DIM),
                                      jnp.transpose(logf, (0, 2, 1)))):
            lst.append(arr)

    p_kv = [jnp.transpose(a, (0, 1, 3, 2, 4)) for a in p_state]
    p_out = [jnp.stack(a, axis=0) for a in p_rest]
    s_out = [jnp.stack(a, axis=0) for a in s_state]
    return (xp.reshape(nb, t, d), xs.reshape(nbs, ts, d), *p_kv, *p_out, *s_out)


def _project_qz(x_bf, nb, w_qz, scale, bf_row, head_major):
    kinds = ["q", "q", "q_tok", "silu", "silu", "silu", "logf"]
    return _project(x_bf, nb, w_qz, kinds, scale, head_major=head_major, bias=bf_row)
```

```python
import functools

import jax
import jax.numpy as jnp
from jax import lax
from jax.experimental import pallas as pl
from jax.experimental.pallas import tpu as pltpu

F32 = jnp.float32
BF16 = jnp.bfloat16
HEAD_DIM = 128
LANES = 128
SUBLANES = 8
LN_EPS = 1e-5
MASK_VALUE = -1e30
LOG2E = 1.4426950408889634
EXP2_ZERO_BELOW = -150.0
VMEM_LIMIT_BYTES = 56 * 1024 * 1024
TOKEN_TILE = 512
Q_SUPER = 1024
SB_BLOCK = 128
FOX_BLOCK = 512
FOX_FIXED_RANGE = 50.0

_NT = (((1,), (1,)), ((), ()))


def _params(*sem):
    return pltpu.CompilerParams(dimension_semantics=sem, vmem_limit_bytes=VMEM_LIMIT_BYTES)


def _tile(n, target):
    for t in range(min(n, target), 0, -1):
        if n % t == 0 and t % SUBLANES == 0:
            return t
    return n


def _sigmoid(x):
    return 1.0 / (1.0 + jnp.exp(-x))


def _softplus(x):
    return jnp.maximum(x, 0.0) + jnp.log(1.0 + jnp.exp(-jnp.abs(x)))


def _proj_kernel(x_ref, *refs, kinds, scale, head_major, n_alias):
    n_seg = len(kinds)
    w_refs = refs[:n_seg]
    refs = refs[n_seg:]
    if "logf" in kinds:
        bf_ref, refs = refs[0], refs[1:]
    out_refs = refs[n_alias:]
    x = x_ref[...]
    oi = 0

    def put(o_ref, val):
        if head_major:
            for h in range(o_ref.shape[0]):
                o_ref[h] = val[:, h * HEAD_DIM:(h + 1) * HEAD_DIM].astype(o_ref.dtype)
        else:
            o_ref[...] = val.astype(o_ref.dtype)

    for w_ref, kind in zip(w_refs, kinds):
        acc = jnp.dot(x, w_ref[...], preferred_element_type=F32)
        if kind == "kv":
            put(out_refs[oi], acc)
            put(out_refs[oi + 1], acc)
            oi += 2
        elif kind in ("q", "q2"):
            put(out_refs[oi], acc * (scale * LOG2E if kind == "q2" else scale))
            oi += 1
        elif kind == "q_tok":
            out_refs[oi][...] = (acc * scale).astype(BF16)
            oi += 1
        elif kind == "logf":
            lf = -_softplus(-(acc + bf_ref[...]))
            out_refs[oi][...] = lf.T[:SUBLANES, :] if head_major else lf
            oi += 1
        else:
            out_refs[oi][...] = (acc * _sigmoid(acc) if kind == "silu" else _sigmoid(acc)).astype(BF16)
            oi += 1


def _project(x_bf, nb, weights, kinds, scale, *, head_major, bias=None, state=None, layer=0):
    n, d = x_bf.shape
    t = n // nb
    tm = _tile(t, TOKEN_TILE)
    tpb = t // tm
    const = lambda a: pl.BlockSpec(a.shape, lambda b, j: (0,) * a.ndim)
    tok = lambda w: pl.BlockSpec((tm, w), lambda b, j: (b * tpb + j, 0))
    in_specs = [tok(d)] + [const(w) for w in weights]
    args = [x_bf, *weights]
    if bias is not None:
        in_specs.append(const(bias))
        args.append(bias)
    out_shape, out_specs, aliases = [], [], {}
    for w, kind in zip(weights, kinds):
        width = w.shape[1]
        heads = width // HEAD_DIM
        if kind in ("kv", "q", "q2") and head_major:
            if kind == "kv":
                depth = state[0]
                out_shape.append(jax.ShapeDtypeStruct((depth, nb, heads, t, HEAD_DIM), F32))
                out_specs.append(pl.BlockSpec((None, None, heads, tm, HEAD_DIM),
                                              lambda b, j: (layer, b, 0, j, 0)))
            out_shape.append(jax.ShapeDtypeStruct((nb, heads, t, HEAD_DIM), BF16))
            out_specs.append(pl.BlockSpec((None, heads, tm, HEAD_DIM), lambda b, j: (b, 0, j, 0)))
        elif kind == "logf" and head_major:
            out_shape.append(jax.ShapeDtypeStruct((nb, SUBLANES, t), F32))
            out_specs.append(pl.BlockSpec((None, SUBLANES, tm), lambda b, j: (b, 0, j)))
        else:
            for dt in {"kv": (F32, BF16), "logf": (F32,)}.get(kind, (BF16,)):
                out_shape.append(jax.ShapeDtypeStruct((n, width), dt))
                out_specs.append(tok(width))
    bufs = [] if state is None or state[1] is None else list(state[1])
    if bufs:
        f32_outs = [i for i, s in enumerate(out_shape) if s.dtype == F32]
        for k, buf in enumerate(bufs):
            aliases[len(args)] = f32_outs[k]
            in_specs.append(pl.BlockSpec(memory_space=pl.ANY))
            args.append(buf)
    return pl.pallas_call(
        functools.partial(_proj_kernel, kinds=tuple(kinds), scale=scale, head_major=head_major,
                          n_alias=len(bufs)),
        grid=(nb, tpb), in_specs=in_specs, out_specs=out_specs, out_shape=out_shape,
        input_output_aliases=aliases,
        compiler_params=_params("parallel", "parallel"), name="project",
    )(*args)


def _cumsum_kernel(x_ref, o_ref):
    x = x_ref[...]
    lane = lax.broadcasted_iota(jnp.int32, x.shape, 1)
    shift = 1
    while shift < x.shape[1]:
        x = x + jnp.where(lane >= shift, pltpu.roll(x, shift, axis=1), 0.0)
        shift *= 2
    o_ref[...] = x


def _cumsum_tokens(logf):
    nb, h, t = logf.shape
    return pl.pallas_call(
        _cumsum_kernel, grid=(nb,),
        in_specs=[pl.BlockSpec((None, h, t), lambda b: (b, 0, 0))],
        out_specs=pl.BlockSpec((None, h, t), lambda b: (b, 0, 0)),
        out_shape=jax.ShapeDtypeStruct(logf.shape, F32),
        compiler_params=_params("parallel"), name="cumsum_tokens",
    )(logf)


def _attention_call(body, name, q, kn, vn, hb, extra, past, scratch):
    nb, nh, tq, _ = q.shape
    tqs = min(tq, Q_SUPER)
    in_specs = [pl.BlockSpec((None, hb, tqs, HEAD_DIM), lambda b, g, s: (b, g, s, 0))]
    in_specs += [pl.BlockSpec((None, hb, kn.shape[2], HEAD_DIM), lambda b, g, s: (b, g, 0, 0))] * 2
    args = [q, kn, vn]
    for a in extra:
        in_specs.append(pl.BlockSpec((None, hb, 1, a.shape[-1]), lambda b, g, s: (b, g, 0, 0)))
        args.append(a)
    if past is not None:
        ck, cv, layer = past
        in_specs += [pl.BlockSpec((None, None, hb, ck.shape[3], HEAD_DIM),
                                  lambda b, g, s: (layer, b, g, 0, 0))] * 2
        args += [ck, cv]
    return pl.pallas_call(
        body, grid=(nb, nh // hb, tq // tqs), in_specs=in_specs,
        out_specs=pl.BlockSpec((None, tqs, hb * HEAD_DIM), lambda b, g, s: (b, s, g)),
        out_shape=jax.ShapeDtypeStruct((nb, tq, nh * HEAD_DIM), BF16),
        scratch_shapes=scratch,
        compiler_params=_params("parallel", "parallel", "arbitrary"), name=name,
    )(*args)


def _sb_step(q_ref, k_ref, v_ref, r0, k0, mask, first, scratch, bq):
    uu_ref, hl_ref, st_ref, ls_ref, acc_ref, r_ref = scratch
    hb = q_ref.shape[0]
    nk = 2 * SB_BLOCK
    for h in range(hb):
        z = lax.dot_general(q_ref[h, pl.ds(r0, bq), :], k_ref[h, pl.ds(k0, nk), :].astype(BF16), _NT,
                            preferred_element_type=F32)
        ls = jnp.minimum(z, 0.0) - jnp.log2(1.0 + jnp.exp2(-jnp.abs(z)))
        log1m = ls - z
        summed = log1m if mask is None else jnp.where(mask, log1m, 0.0)
        hi = summed.astype(BF16)
        lo = (summed - hi.astype(F32)).astype(BF16)
        for sub in range(2):
            rows = pl.ds((2 * h + sub) * bq, bq)
            hl_ref[rows, :SB_BLOCK] = hi[:, sub * SB_BLOCK:(sub + 1) * SB_BLOCK]
            hl_ref[rows, SB_BLOCK:] = lo[:, sub * SB_BLOCK:(sub + 1) * SB_BLOCK]
        ls_ref[h] = ls
    st_ref[...] = jnp.dot(hl_ref[...], uu_ref[...], preferred_element_type=F32)
    top = None
    for h in range(hb):
        s0 = st_ref[pl.ds((2 * h) * bq, bq), :]
        s1 = st_ref[pl.ds((2 * h + 1) * bq, bq), :]
        later = s1[:, SB_BLOCK:]
        tail1, tail0 = s1[:, :SB_BLOCK], s0[:, :SB_BLOCK] + later
        total = s0[:, SB_BLOCK:] + later
        if not first:
            carry = r_ref[h]
            tail1, tail0, total = tail1 + carry, tail0 + carry, total + carry
        w = jnp.exp2(ls_ref[h] + jnp.concatenate([tail0, tail1], axis=1))
        if mask is not None:
            w = jnp.where(mask, w, 0.0)
        pv = jnp.dot(w.astype(BF16), v_ref[h, pl.ds(k0, nk), :].astype(BF16), preferred_element_type=F32)
        if first:
            acc_ref[h] = pv
        else:
            acc_ref[h] += pv
        r_ref[h] = total
        top = total if top is None else jnp.maximum(top, total)
    return jnp.max(top)


def _sb_kernel(*refs, bq, has_past):
    if has_past:
        q_ref, kn_ref, vn_ref, kp_ref, vp_ref, o_ref, *scratch = refs
    else:
        q_ref, kn_ref, vn_ref, o_ref, *scratch = refs
        kp_ref, vp_ref = kn_ref, vn_ref
    uu_ref, acc_ref = scratch[0], scratch[4]
    hb, tqs, _ = q_ref.shape
    nk = 2 * SB_BLOCK
    base = pl.program_id(2) * tqs

    key = lax.broadcasted_iota(jnp.int32, uu_ref.shape, 0) % SB_BLOCK
    col = lax.broadcasted_iota(jnp.int32, uu_ref.shape, 1)
    uu_ref[...] = jnp.where((col >= SB_BLOCK) | (key > col), 1.0, 0.0).astype(BF16)
    strict = (lax.broadcasted_iota(jnp.int32, (bq, nk), 1) < lax.broadcasted_iota(jnp.int32, (bq, nk), 0))

    def q_block(i, _):
        r0 = pl.multiple_of(i * bq, bq)
        d0 = 0 if has_past else pl.multiple_of(base + r0, nk)
        top = _sb_step(q_ref, kn_ref, vn_ref, r0, d0, strict, True, scratch, bq)
        n_past = kp_ref.shape[1] // nk if has_past else (base + r0) // nk

        def cond(c):
            return (c[0] >= 0) & (c[1] > EXP2_ZERO_BELOW)

        def body(c):
            k0 = pl.multiple_of(c[0] * nk, nk)
            return c[0] - 1, _sb_step(q_ref, kp_ref, vp_ref, r0, k0, None, False, scratch, bq)

        lax.while_loop(cond, body, (n_past - 1, top))
        for h in range(hb):
            o_ref[pl.ds(r0, bq), h * HEAD_DIM:(h + 1) * HEAD_DIM] = acc_ref[h].astype(o_ref.dtype)
        return 0

    lax.fori_loop(0, tqs // bq, q_block, 0)


def _stick_breaking(q, kn, vn, hb, past=None):
    nk = 2 * SB_BLOCK
    bq = min(q.shape[2], nk)
    scratch = [pltpu.VMEM((nk, SB_BLOCK + LANES), BF16),
               pltpu.VMEM((2 * hb * bq, nk), BF16),
               pltpu.VMEM((2 * hb * bq, SB_BLOCK + LANES), F32),
               pltpu.VMEM((hb, bq, nk), F32),
               pltpu.VMEM((hb, bq, HEAD_DIM), F32),
               pltpu.VMEM((hb, bq, LANES), F32)]
    body = functools.partial(_sb_kernel, bq=bq, has_past=past is not None)
    return _attention_call(body, "stick_breaking", q, kn, vn, hb, (), past, scratch)


def _fox_kernel(*refs, bq, bk, has_past):
    if has_past:
        q_ref, kn_ref, vn_ref, f_ref, kp_ref, vp_ref, o_ref = refs
        new_off = kp_ref.shape[1]
    else:
        q_ref, kn_ref, vn_ref, f_ref, o_ref = refs
        kp_ref, vp_ref = kn_ref, vn_ref
        new_off = 0
    hb, tqs, _ = q_ref.shape
    bd = min(kn_ref.shape[1], bk)
    base = pl.program_id(2) * tqs
    causal = (lax.broadcasted_iota(jnp.int32, (bq, bd), 1) <= lax.broadcasted_iota(jnp.int32, (bq, bd), 0))

    def q_block(i, _):
        r0 = pl.multiple_of(i * bq, bq)
        d0 = 0 if has_past else pl.multiple_of(base + r0, bd)
        n_past = kp_ref.shape[1] // bk if has_past else (base + r0) // bk
        state, f0s = [], []
        for h in range(hb):
            fd = f_ref[h, :, pl.ds(new_off + d0, bd)]
            f0 = fd[:, 0:1]
            z = lax.dot_general(q_ref[h, pl.ds(r0, bq), :], kn_ref[h, pl.ds(d0, bd), :].astype(BF16), _NT,
                                preferred_element_type=F32)
            z = jnp.where(causal, z - (fd - f0) * LOG2E, MASK_VALUE)
            m = jnp.max(z, axis=1, keepdims=True)
            p = jnp.exp2(z - m)
            l = jnp.sum(p, axis=1, keepdims=True)
            acc = jnp.dot(p.astype(BF16), vn_ref[h, pl.ds(d0, bd), :].astype(BF16), preferred_element_type=F32)
            state += [m, l, acc]
            f0s.append(f0)

        def logits(h, s0):
            z = lax.dot_general(q_ref[h, pl.ds(r0, bq), :], kp_ref[h, pl.ds(s0, bk), :].astype(BF16), _NT,
                                preferred_element_type=F32)
            return z - (f_ref[h, :, pl.ds(s0, bk)] - f0s[h]) * LOG2E

        def online_body(j, carry):
            s0 = pl.multiple_of((n_past - 1 - j) * bk, bk)
            out = []
            for h in range(hb):
                m, l, acc = carry[3 * h:3 * h + 3]
                z = logits(h, s0)
                m_new = jnp.maximum(m, jnp.max(z, axis=1, keepdims=True))
                a = jnp.exp2(m - m_new)
                p = jnp.exp2(z - m_new)
                l = a * l + jnp.sum(p, axis=1, keepdims=True)
                acc = a * acc + jnp.dot(p.astype(BF16), vp_ref[h, pl.ds(s0, bk), :].astype(BF16),
                                        preferred_element_type=F32)
                out += [m_new, l, acc]
            return tuple(out)

        def fixed_body(j, carry):
            s0 = pl.multiple_of((n_past - 1 - j) * bk, bk)
            out = []
            for h in range(hb):
                l, acc = carry[2 * h:2 * h + 2]
                p = jnp.exp2(logits(h, s0) - state[3 * h])
                l = l + jnp.sum(p, axis=1, keepdims=True)
                acc = acc + jnp.dot(p.astype(BF16), vp_ref[h, pl.ds(s0, bk), :].astype(BF16),
                                    preferred_element_type=F32)
                out += [l, acc]
            return tuple(out)

        def finish(ls, accs):
            for h in range(hb):
                o_ref[pl.ds(r0, bq), h * HEAD_DIM:(h + 1) * HEAD_DIM] = (accs[h] / ls[h]).astype(o_ref.dtype)

        def online_walk():
            out = lax.fori_loop(0, n_past, online_body, tuple(state))
            finish(out[1::3], out[2::3])

        def fixed_walk():
            init = tuple(x for h in range(hb) for x in state[3 * h + 1:3 * h + 3])
            out = lax.fori_loop(0, n_past, fixed_body, init)
            finish(out[0::2], out[1::2])

        if has_past:
            online_walk()
        else:
            safe = None
            for h in range(hb):
                qf = q_ref[h, pl.ds(r0, bq), :].astype(F32)
                ok = jnp.max(jnp.sum(qf * qf, axis=1, keepdims=True)) * ksq[h] <= FOX_FIXED_RANGE ** 2
                safe = ok if safe is None else safe & ok
            pl.when(safe)(fixed_walk)
            pl.when(jnp.logical_not(safe))(online_walk)
        return 0

    ksq = None if has_past else [_max_sq_norm(kp_ref, h) for h in range(hb)]
    lax.fori_loop(0, tqs // bq, q_block, 0)


def _max_sq_norm(ref, h):
    rows = ref.shape[1]
    chunk = min(rows, FOX_BLOCK)

    def body(j, top):
        x = ref[h, pl.ds(pl.multiple_of(j * chunk, chunk), chunk), :].astype(F32)
        return jnp.maximum(top, jnp.max(jnp.sum(x * x, axis=1, keepdims=True)))

    return lax.fori_loop(0, rows // chunk, body, jnp.float32(0.0))


def _forgetting(q, kn, vn, f_cum, hb, past=None):
    bq = min(q.shape[2], FOX_BLOCK)
    body = functools.partial(_fox_kernel, bq=bq, bk=FOX_BLOCK, has_past=past is not None)
    return _attention_call(body, "forgetting", q, kn, vn, hb, (f_cum,), past, [])


def _mem_kernel(q_ref, mk_ref, mv_ref, o_ref):
    for h in range(q_ref.shape[1] // HEAD_DIM):
        sl = slice(h * HEAD_DIM, (h + 1) * HEAD_DIM)
        z = lax.dot_general(q_ref[:, sl], mk_ref[:, sl].astype(BF16), _NT, preferred_element_type=F32)
        p = jnp.exp(z - jnp.max(z, axis=1, keepdims=True))
        o = jnp.dot(p.astype(BF16), mv_ref[:, sl].astype(BF16), preferred_element_type=F32)
        o_ref[:, sl] = (o / jnp.sum(p, axis=1, keepdims=True)).astype(o_ref.dtype)


def _memory_attention(q, mk, mv, lead=()):
    nb, t, w = q.shape
    tq = _tile(t, TOKEN_TILE)
    mem_spec = pl.BlockSpec((None,) * (len(lead) + 1) + mk.shape[-2:], lambda b, j: (*lead, b, 0, 0))
    return pl.pallas_call(
        _mem_kernel, grid=(nb, t // tq),
        in_specs=[pl.BlockSpec((None, tq, w), lambda b, j: (b, j, 0)), mem_spec, mem_spec],
        out_specs=pl.BlockSpec((None, tq, w), lambda b, j: (b, j, 0)),
        out_shape=jax.ShapeDtypeStruct(q.shape, BF16),
        compiler_params=_params("parallel", "parallel"), name="memory_attention",
    )(q, mk, mv)


def _merge_kernel(x_ref, o_sb, o_fx, o_mem, z_sb, z_fx, z_mem, g_sb, g_fx, g_mem,
                  w_sb, w_fx, w_mem, w_out, ln_g, ln_b, y_ref, ybf_ref, *, alpha):
    def branch(o_ref, z_ref, g_ref, w_ref):
        y = jnp.dot(o_ref[...] * z_ref[...], w_ref[...], preferred_element_type=F32)
        return g_ref[...].astype(F32) * y

    merged = branch(o_sb, z_sb, g_sb, w_sb) + branch(o_fx, z_fx, g_fx, w_fx) + branch(o_mem, z_mem, g_mem, w_mem)
    r = alpha * x_ref[...] + jnp.dot(merged.astype(BF16), w_out[...], preferred_element_type=F32)
    c = r - jnp.mean(r, axis=-1, keepdims=True)
    var = jnp.mean(c * c, axis=-1, keepdims=True)
    y = c * lax.rsqrt(var + LN_EPS) * ln_g[...] + ln_b[...]
    y_ref[...] = y
    ybf_ref[...] = y.astype(BF16)


def _merge(x, acts, weights, ln_g, ln_b, alpha):
    n, d = x.shape
    tm = _tile(n, TOKEN_TILE)
    tok = lambda a: pl.BlockSpec((tm, a.shape[1]), lambda i: (i, 0))
    full = lambda a: pl.BlockSpec(a.shape, lambda i: (0, 0))
    return pl.pallas_call(
        functools.partial(_merge_kernel, alpha=alpha), grid=(n // tm,),
        in_specs=[tok(x)] + [tok(a) for a in acts] + [full(w) for w in weights] + [full(ln_g), full(ln_b)],
        out_specs=[tok(x), tok(x)],
        out_shape=[jax.ShapeDtypeStruct((n, d), F32), jax.ShapeDtypeStruct((n, d), BF16)],
        compiler_params=_params("parallel"), name="merge",
    )(x, *acts, *weights, ln_g, ln_b)


def _head_group(n_heads, target):
    return max(g for g in range(1, min(n_heads, target) + 1) if n_heads % g == 0)


def kernel(x_prompt, x_sample, cache_sb_k, cache_sb_v, cache_fox_k, cache_fox_v, cache_fox_logf,
           cache_mem_k, cache_mem_v, mem_prompt, w_in, b_f, w_mem_kv, w_br_sb, w_br_fox, w_br_mem,
           w_out, ln_g, ln_b):
    depth, d, _ = w_in.shape
    nb, t, _ = x_prompt.shape
    nbs, ts, _ = x_sample.shape
    h_sb, h_fx, h_mem = cache_sb_k.shape[3], cache_fox_k.shape[3], cache_mem_k.shape[3]
    w_sb, w_fx, w_mem = h_sb * HEAD_DIM, h_fx * HEAD_DIM, h_mem * HEAD_DIM
    n_mem = mem_prompt.shape[1]
    alpha = float((2 * depth) ** 0.25)
    scale = HEAD_DIM ** -0.5
    ts_pad = -(-ts // (2 * SB_BLOCK)) * (2 * SB_BLOCK)

    sizes = [w_sb] * 4 + [w_fx] * 4 + [h_fx] + [w_mem] * 2 + [d] * 3
    offs = [0]
    for s in sizes:
        offs.append(offs[-1] + s)
    (c_qsb, c_ksb, c_vsb, c_zsb, c_qfx, c_kfx, c_vfx, c_zfx, c_f, c_qm, c_zm, c_gsb, c_gfx, c_gm) = [
        (offs[i], offs[i + 1]) for i in range(len(sizes))]

    w_in_bf = w_in.astype(BF16)
    w_mem_kv_bf = w_mem_kv.astype(BF16)
    w_br_bf = [w.astype(BF16) for w in (w_br_sb, w_br_fox, w_br_mem, w_out)]
    c_sb_k, c_sb_v, c_fx_k, c_fx_v = [jnp.transpose(c, (0, 1, 3, 2, 4))
                                      for c in (cache_sb_k, cache_sb_v, cache_fox_k, cache_fox_v)]
    c_mk, c_mv = [c.reshape(*c.shape[:3], -1) for c in (cache_mem_k, cache_mem_v)]
    c_logf = jnp.transpose(cache_fox_logf.astype(F32), (0, 1, 3, 2))
    mem_bf = mem_prompt.reshape(nb * n_mem, d).astype(BF16)

    xp, xs = x_prompt.reshape(nb * t, d), x_sample.reshape(nbs * ts, d)
    xp_bf, xs_bf = xp.astype(BF16), xs.astype(BF16)
    p_state = None
    p_rest = [[] for _ in range(3)]
    s_state = [[] for _ in range(5)]
    hb_sb_p, hb_fx_p = _head_group(h_sb, 6), _head_group(h_fx, 3)
    hb_sb_s, hb_fx_s = _head_group(h_sb, 3), _head_group(h_fx, 3)

    for l in range(depth):
        wl = lambda c: w_in_bf[l, :, c[0]:c[1]]
        w_kv = [wl(c) for c in (c_ksb, c_vsb, c_kfx, c_vfx)]
        w_qz = [wl(c) for c in (c_qsb, c_qfx, c_qm, c_zsb, c_zfx, c_zm)]
        w_qz.append(jnp.pad(wl(c_f), ((0, 0), (0, LANES - h_fx))))
        w_g = [wl(c) for c in (c_gsb, c_gfx, c_gm)]
        bf_row = jnp.pad(b_f[l].astype(F32), (0, LANES - h_fx)).reshape(1, LANES)
        merge_w = [w[l] for w in w_br_bf]
        g_row, b_row = ln_g[l].astype(F32).reshape(1, d), ln_b[l].astype(F32).reshape(1, d)

        kv = _project(xp_bf, nb, w_kv, ["kv"] * 4, scale, head_major=True, state=(depth, p_state), layer=l)
        p_state = kv[0::2]
        k_sb_bf, v_sb_bf, k_fx_bf, v_fx_bf = kv[1::2]
        q_sb, q_fx, q_m, z_sb, z_fx, z_m, logf = _project_qz(xp_bf, nb, w_qz, scale, bf_row, True)
        gates = _project(xp_bf, nb, w_g, ["sigmoid"] * 3, scale, head_major=True)
        f_cum = _cumsum_tokens(logf)[:, :h_fx, None, :]
        mk, mk_bf, mv, mv_bf = _project(mem_bf, nb, [w_mem_kv_bf[l, :, :w_mem], w_mem_kv_bf[l, :, w_mem:]],
                                        ["kv"] * 2, scale, head_major=False)
        o_sb = _stick_breaking(q_sb, k_sb_bf, v_sb_bf, hb_sb_p)
        o_fx = _forgetting(q_fx, k_fx_bf, v_fx_bf, f_cum, hb_fx_p)
        o_m = _memory_attention(q_m.reshape(nb, t, w_mem), mk_bf.reshape(nb, n_mem, w_mem),
                                mv_bf.reshape(nb, n_mem, w_mem))
        f2 = lambda a: a.reshape(nb * t, -1)
        xp, xp_bf = _merge(xp, [f2(o_sb), f2(o_fx), f2(o_m), z_sb, z_fx, z_m, *gates], merge_w, g_row, b_row, alpha)
        for lst, arr in zip(p_rest, (jnp.transpose(logf[:, :h_fx, :], (0, 2, 1)),
                                     mk.reshape(nb, n_mem, h_mem, HEAD_DIM), mv.reshape(nb, n_mem, h_mem, HEAD_DIM))):
            lst.append(arr)

        k_sb, k_sb_bf, v_sb, v_sb_bf, k_fx, k_fx_bf, v_fx, v_fx_bf = _project(
            xs_bf, 1, w_kv, ["kv"] * 4, scale, head_major=False)
        q_sb, q_fx, q_m, z_sb, z_fx, z_m, logf = _project_qz(xs_bf, 1, w_qz, scale, bf_row, False)
        gates = _project(xs_bf, 1, w_g, ["sigmoid"] * 3, scale, head_major=False)
        logf = jnp.transpose(logf[:, :h_fx].reshape(nbs, ts, h_fx), (0, 2, 1))
        f_all = jnp.concatenate([c_logf[l], jnp.pad(logf, ((0, 0), (0, 0), (0, ts_pad - ts)))], axis=-1)
        f_all = jnp.pad(f_all, ((0, 0), (0, SUBLANES - h_fx), (0, 0)))
        f_cum = _cumsum_tokens(f_all)[:, :h_fx, None, :]
        hm = lambda a: jnp.transpose(a.reshape(nbs, ts, -1, HEAD_DIM), (0, 2, 1, 3))
        hm_pad = lambda a: jnp.pad(hm(a), ((0, 0), (0, 0), (0, ts_pad - ts), (0, 0)))
        o_sb = _stick_breaking(hm(q_sb), hm_pad(k_sb_bf), hm_pad(v_sb_bf), hb_sb_s, past=(c_sb_k, c_sb_v, l))
        o_fx = _forgetting(hm(q_fx), hm_pad(k_fx_bf), hm_pad(v_fx_bf), f_cum, hb_fx_s, past=(c_fx_k, c_fx_v, l))
        o_m = _memory_attention(q_m.reshape(nbs, ts, w_mem), c_mk, c_mv, lead=(l,))
        f2 = lambda a: a.reshape(nbs * ts, -1)
        xs, xs_bf = _merge(xs, [f2(o_sb), f2(o_fx), f2(o_m), z_sb, z_fx, z_m, *gates], merge_w, g_row, b_row, alpha)
        for lst, arr in zip(s_state, (k_sb.reshape(nbs, ts, h_sb, HEAD_DIM), v_sb.reshape(nbs, ts, h_sb, HEAD_DIM),
                                      k_fx.reshape(nbs, ts, h_fx, HEAD_DIM), v_fx.reshape(nbs, ts, h_fx, HEAD_DIM),
                                      jnp.transpose(logf, (0, 2, 1)))):
            lst.append(arr)

    p_kv = [jnp.transpose(a, (0, 1, 3, 2, 4)) for a in p_state]
    p_out = [jnp.stack(a, axis=0) for a in p_rest]
    s_out = [jnp.stack(a, axis=0) for a in s_state]
    return (xp.reshape(nb, t, d), xs.reshape(nbs, ts, d), *p_kv, *p_out, *s_out)


def _project_qz(x_bf, nb, w_qz, scale, bf_row, head_major):
    kinds = ["q2", "q2", "q_tok", "silu", "silu", "silu", "logf"]
    return _project(x_bf, nb, w_qz, kinds, scale, head_major=head_major, bias=bf_row)
```

```python
import functools

import jax
import jax.numpy as jnp
from jax import lax
from jax.experimental import pallas as pl
from jax.experimental.pallas import tpu as pltpu

F32 = jnp.float32
BF16 = jnp.bfloat16
HEAD_DIM = 128
LANES = 128
SUBLANES = 8
LN_EPS = 1e-5
MASK_VALUE = -1e30
LOG2E = 1.4426950408889634
EXP2_ZERO_BELOW = -150.0
VMEM_LIMIT_BYTES = 56 * 1024 * 1024
TOKEN_TILE = 512
Q_SUPER = 1024
SB_BLOCK = 128
FOX_BLOCK = 512
FOX_FIXED_RANGE = 50.0
FOX_SKIP_BELOW = -152.0

_NT = (((1,), (1,)), ((), ()))


def _params(*sem):
    return pltpu.CompilerParams(dimension_semantics=sem, vmem_limit_bytes=VMEM_LIMIT_BYTES)


def _tile(n, target):
    for t in range(min(n, target), 0, -1):
        if n % t == 0 and t % SUBLANES == 0:
            return t
    return n


def _sigmoid(x):
    return 1.0 / (1.0 + jnp.exp(-x))


def _softplus(x):
    return jnp.maximum(x, 0.0) + jnp.log(1.0 + jnp.exp(-jnp.abs(x)))


def _proj_kernel(x_ref, *refs, kinds, scale, head_major, n_alias):
    n_seg = len(kinds)
    w_refs = refs[:n_seg]
    refs = refs[n_seg:]
    if "logf" in kinds:
        bf_ref, refs = refs[0], refs[1:]
    out_refs = refs[n_alias:]
    x = x_ref[...]
    oi = 0

    def put(o_ref, val):
        if head_major:
            for h in range(o_ref.shape[0]):
                o_ref[h] = val[:, h * HEAD_DIM:(h + 1) * HEAD_DIM].astype(o_ref.dtype)
        else:
            o_ref[...] = val.astype(o_ref.dtype)

    for w_ref, kind in zip(w_refs, kinds):
        acc = jnp.dot(x, w_ref[...], preferred_element_type=F32)
        if kind == "kv":
            put(out_refs[oi], acc)
            put(out_refs[oi + 1], acc)
            oi += 2
        elif kind in ("q", "q2"):
            put(out_refs[oi], acc * (scale * LOG2E if kind == "q2" else scale))
            oi += 1
        elif kind == "q_tok":
            out_refs[oi][...] = (acc * scale).astype(BF16)
            oi += 1
        elif kind == "logf":
            lf = -_softplus(-(acc + bf_ref[...]))
            out_refs[oi][...] = lf.T[:SUBLANES, :] if head_major else lf
            oi += 1
        else:
            out_refs[oi][...] = (acc * _sigmoid(acc) if kind == "silu" else _sigmoid(acc)).astype(BF16)
            oi += 1


def _project(x_bf, nb, weights, kinds, scale, *, head_major, bias=None, state=None, layer=0):
    n, d = x_bf.shape
    t = n // nb
    tm = _tile(t, TOKEN_TILE)
    tpb = t // tm
    const = lambda a: pl.BlockSpec(a.shape, lambda b, j: (0,) * a.ndim)
    tok = lambda w: pl.BlockSpec((tm, w), lambda b, j: (b * tpb + j, 0))
    in_specs = [tok(d)] + [const(w) for w in weights]
    args = [x_bf, *weights]
    if bias is not None:
        in_specs.append(const(bias))
        args.append(bias)
    out_shape, out_specs, aliases = [], [], {}
    for w, kind in zip(weights, kinds):
        width = w.shape[1]
        heads = width // HEAD_DIM
        if kind in ("kv", "q", "q2") and head_major:
            if kind == "kv":
                depth = state[0]
                out_shape.append(jax.ShapeDtypeStruct((depth, nb, heads, t, HEAD_DIM), F32))
                out_specs.append(pl.BlockSpec((None, None, heads, tm, HEAD_DIM),
                                              lambda b, j: (layer, b, 0, j, 0)))
            out_shape.append(jax.ShapeDtypeStruct((nb, heads, t, HEAD_DIM), BF16))
            out_specs.append(pl.BlockSpec((None, heads, tm, HEAD_DIM), lambda b, j: (b, 0, j, 0)))
        elif kind == "logf" and head_major:
            out_shape.append(jax.ShapeDtypeStruct((nb, SUBLANES, t), F32))
            out_specs.append(pl.BlockSpec((None, SUBLANES, tm), lambda b, j: (b, 0, j)))
        else:
            for dt in {"kv": (F32, BF16), "logf": (F32,)}.get(kind, (BF16,)):
                out_shape.append(jax.ShapeDtypeStruct((n, width), dt))
                out_specs.append(tok(width))
    bufs = [] if state is None or state[1] is None else list(state[1])
    if bufs:
        f32_outs = [i for i, s in enumerate(out_shape) if s.dtype == F32]
        for k, buf in enumerate(bufs):
            aliases[len(args)] = f32_outs[k]
            in_specs.append(pl.BlockSpec(memory_space=pl.ANY))
            args.append(buf)
    return pl.pallas_call(
        functools.partial(_proj_kernel, kinds=tuple(kinds), scale=scale, head_major=head_major,
                          n_alias=len(bufs)),
        grid=(nb, tpb), in_specs=in_specs, out_specs=out_specs, out_shape=out_shape,
        input_output_aliases=aliases,
        compiler_params=_params("parallel", "parallel"), name="project",
    )(*args)


def _cumsum_kernel(x_ref, o_ref):
    x = x_ref[...]
    lane = lax.broadcasted_iota(jnp.int32, x.shape, 1)
    shift = 1
    while shift < x.shape[1]:
        x = x + jnp.where(lane >= shift, pltpu.roll(x, shift, axis=1), 0.0)
        shift *= 2
    o_ref[...] = x


def _cumsum_tokens(logf):
    nb, h, t = logf.shape
    return pl.pallas_call(
        _cumsum_kernel, grid=(nb,),
        in_specs=[pl.BlockSpec((None, h, t), lambda b: (b, 0, 0))],
        out_specs=pl.BlockSpec((None, h, t), lambda b: (b, 0, 0)),
        out_shape=jax.ShapeDtypeStruct(logf.shape, F32),
        compiler_params=_params("parallel"), name="cumsum_tokens",
    )(logf)


def _attention_call(body, name, q, kn, vn, hb, extra, past, scratch, smem=()):
    nb, nh, tq, _ = q.shape
    tqs = min(tq, Q_SUPER)
    in_specs = [pl.BlockSpec((None, hb, tqs, HEAD_DIM), lambda b, g, s: (b, g, s, 0))]
    in_specs += [pl.BlockSpec((None, hb, kn.shape[2], HEAD_DIM), lambda b, g, s: (b, g, 0, 0))] * 2
    args = [q, kn, vn]
    for a in extra:
        in_specs.append(pl.BlockSpec((None, hb, 1, a.shape[-1]), lambda b, g, s: (b, g, 0, 0)))
        args.append(a)
    for a in smem:
        in_specs.append(pl.BlockSpec(memory_space=pltpu.SMEM))
        args.append(a)
    if past is not None:
        ck, cv, layer = past
        in_specs += [pl.BlockSpec((None, None, hb, ck.shape[3], HEAD_DIM),
                                  lambda b, g, s: (layer, b, g, 0, 0))] * 2
        args += [ck, cv]
    return pl.pallas_call(
        body, grid=(nb, nh // hb, tq // tqs), in_specs=in_specs,
        out_specs=pl.BlockSpec((None, tqs, hb * HEAD_DIM), lambda b, g, s: (b, s, g)),
        out_shape=jax.ShapeDtypeStruct((nb, tq, nh * HEAD_DIM), BF16),
        scratch_shapes=scratch,
        compiler_params=_params("parallel", "parallel", "arbitrary"), name=name,
    )(*args)


def _sb_step(q_ref, k_ref, v_ref, r0, k0, mask, first, scratch, bq):
    uu_ref, hl_ref, st_ref, ls_ref, acc_ref, r_ref = scratch
    hb = q_ref.shape[0]
    nk = 2 * SB_BLOCK
    for h in range(hb):
        z = lax.dot_general(q_ref[h, pl.ds(r0, bq), :], k_ref[h, pl.ds(k0, nk), :].astype(BF16), _NT,
                            preferred_element_type=F32)
        ls = jnp.minimum(z, 0.0) - jnp.log2(1.0 + jnp.exp2(-jnp.abs(z)))
        log1m = ls - z
        summed = log1m if mask is None else jnp.where(mask, log1m, 0.0)
        hi = summed.astype(BF16)
        lo = (summed - hi.astype(F32)).astype(BF16)
        for sub in range(2):
            rows = pl.ds((2 * h + sub) * bq, bq)
            hl_ref[rows, :SB_BLOCK] = hi[:, sub * SB_BLOCK:(sub + 1) * SB_BLOCK]
            hl_ref[rows, SB_BLOCK:] = lo[:, sub * SB_BLOCK:(sub + 1) * SB_BLOCK]
        ls_ref[h] = ls
    st_ref[...] = jnp.dot(hl_ref[...], uu_ref[...], preferred_element_type=F32)
    top = None
    for h in range(hb):
        s0 = st_ref[pl.ds((2 * h) * bq, bq), :]
        s1 = st_ref[pl.ds((2 * h + 1) * bq, bq), :]
        later = s1[:, SB_BLOCK:]
        tail1, tail0 = s1[:, :SB_BLOCK], s0[:, :SB_BLOCK] + later
        total = s0[:, SB_BLOCK:] + later
        if not first:
            carry = r_ref[h]
            tail1, tail0, total = tail1 + carry, tail0 + carry, total + carry
        w = jnp.exp2(ls_ref[h] + jnp.concatenate([tail0, tail1], axis=1))
        if mask is not None:
            w = jnp.where(mask, w, 0.0)
        pv = jnp.dot(w.astype(BF16), v_ref[h, pl.ds(k0, nk), :].astype(BF16), preferred_element_type=F32)
        if first:
            acc_ref[h] = pv
        else:
            acc_ref[h] += pv
        r_ref[h] = total
        top = total if top is None else jnp.maximum(top, total)
    return jnp.max(top)


def _sb_kernel(*refs, bq, has_past):
    if has_past:
        q_ref, kn_ref, vn_ref, kp_ref, vp_ref, o_ref, *scratch = refs
    else:
        q_ref, kn_ref, vn_ref, o_ref, *scratch = refs
        kp_ref, vp_ref = kn_ref, vn_ref
    uu_ref, acc_ref = scratch[0], scratch[4]
    hb, tqs, _ = q_ref.shape
    nk = 2 * SB_BLOCK
    base = pl.program_id(2) * tqs

    key = lax.broadcasted_iota(jnp.int32, uu_ref.shape, 0) % SB_BLOCK
    col = lax.broadcasted_iota(jnp.int32, uu_ref.shape, 1)
    uu_ref[...] = jnp.where((col >= SB_BLOCK) | (key > col), 1.0, 0.0).astype(BF16)
    strict = (lax.broadcasted_iota(jnp.int32, (bq, nk), 1) < lax.broadcasted_iota(jnp.int32, (bq, nk), 0))

    def q_block(i, _):
        r0 = pl.multiple_of(i * bq, bq)
        d0 = 0 if has_past else pl.multiple_of(base + r0, nk)
        top = _sb_step(q_ref, kn_ref, vn_ref, r0, d0, strict, True, scratch, bq)
        n_past = kp_ref.shape[1] // nk if has_past else (base + r0) // nk

        def cond(c):
            return (c[0] >= 0) & (c[1] > EXP2_ZERO_BELOW)

        def body(c):
            k0 = pl.multiple_of(c[0] * nk, nk)
            return c[0] - 1, _sb_step(q_ref, kp_ref, vp_ref, r0, k0, None, False, scratch, bq)

        lax.while_loop(cond, body, (n_past - 1, top))
        for h in range(hb):
            o_ref[pl.ds(r0, bq), h * HEAD_DIM:(h + 1) * HEAD_DIM] = acc_ref[h].astype(o_ref.dtype)
        return 0

    lax.fori_loop(0, tqs // bq, q_block, 0)


def _stick_breaking(q, kn, vn, hb, past=None):
    nk = 2 * SB_BLOCK
    bq = min(q.shape[2], nk)
    scratch = [pltpu.VMEM((nk, SB_BLOCK + LANES), BF16),
               pltpu.VMEM((2 * hb * bq, nk), BF16),
               pltpu.VMEM((2 * hb * bq, SB_BLOCK + LANES), F32),
               pltpu.VMEM((hb, bq, nk), F32),
               pltpu.VMEM((hb, bq, HEAD_DIM), F32),
               pltpu.VMEM((hb, bq, LANES), F32)]
    body = functools.partial(_sb_kernel, bq=bq, has_past=past is not None)
    return _attention_call(body, "stick_breaking", q, kn, vn, hb, (), past, scratch)


def _fox_kernel(*refs, bq, bk, has_past, n_heads):
    if has_past:
        q_ref, kn_ref, vn_ref, f_ref, kp_ref, vp_ref, o_ref = refs
        new_off = kp_ref.shape[1]
    else:
        q_ref, kn_ref, vn_ref, f_ref, fend_ref, o_ref = refs
        kp_ref, vp_ref = kn_ref, vn_ref
        new_off = 0
    hb, tqs, _ = q_ref.shape
    bd = min(kn_ref.shape[1], bk)
    base = pl.program_id(2) * tqs
    causal = (lax.broadcasted_iota(jnp.int32, (bq, bd), 1) <= lax.broadcasted_iota(jnp.int32, (bq, bd), 0))

    def q_block(i, _):
        r0 = pl.multiple_of(i * bq, bq)
        d0 = 0 if has_past else pl.multiple_of(base + r0, bd)
        n_past = kp_ref.shape[1] // bk if has_past else (base + r0) // bk
        f_diag = [f_ref[h, :, pl.ds(new_off + d0, bd)] for h in range(hb)]
        f0s = [fd[:, 0:1] for fd in f_diag]
        frel = [(fd - f0) * LOG2E for fd, f0 in zip(f_diag, f0s)]

        def diag_logits(h):
            z = lax.dot_general(q_ref[h, pl.ds(r0, bq), :], kn_ref[h, pl.ds(d0, bd), :].astype(BF16), _NT,
                                preferred_element_type=F32)
            return z - frel[h]

        def past_logits(h, s0):
            z = lax.dot_general(q_ref[h, pl.ds(r0, bq), :], kp_ref[h, pl.ds(s0, bk), :].astype(BF16), _NT,
                                preferred_element_type=F32)
            return z - (f_ref[h, :, pl.ds(s0, bk)] - f0s[h]) * LOG2E

        def weigh(p, v_ref, h, s0, rows):
            return (jnp.sum(p, axis=1, keepdims=True),
                    jnp.dot(p.astype(BF16), v_ref[h, pl.ds(s0, rows), :].astype(BF16),
                            preferred_element_type=F32))

        def finish(ls, accs):
            for h in range(hb):
                o_ref[pl.ds(r0, bq), h * HEAD_DIM:(h + 1) * HEAD_DIM] = (accs[h] / ls[h]).astype(o_ref.dtype)

        def online_walk():
            state = []
            for h in range(hb):
                z = jnp.where(causal, diag_logits(h), MASK_VALUE)
                m = jnp.max(z, axis=1, keepdims=True)
                l, acc = weigh(jnp.exp2(z - m), vn_ref, h, d0, bd)
                state += [m, l, acc]

            def body(j, carry):
                s0 = pl.multiple_of((n_past - 1 - j) * bk, bk)
                out = []
                for h in range(hb):
                    m, l, acc = carry[3 * h:3 * h + 3]
                    z = past_logits(h, s0)
                    m_new = jnp.maximum(m, jnp.max(z, axis=1, keepdims=True))
                    a = jnp.exp2(m - m_new)
                    dl, dacc = weigh(jnp.exp2(z - m_new), vp_ref, h, s0, bk)
                    out += [m_new, a * l + dl, a * acc + dacc]
                return tuple(out)

            out = lax.fori_loop(0, n_past, body, tuple(state))
            finish(out[1::3], out[2::3])

        def fixed_walk():
            ms, state = [], []
            for h in range(hb):
                qf = q_ref[h, pl.ds(r0, bq), :].astype(F32)
                reach = jnp.sqrt(jnp.sum(qf * qf, axis=1, keepdims=True) * ksq[h])
                f_rows = jnp.transpose(jnp.broadcast_to(frel[h], (LANES, bd)))[:, 0:1]
                m = -f_rows - reach
                p = jnp.where(causal, jnp.exp2(diag_logits(h) - m), 0.0)
                ms.append(m)
                state += list(weigh(p, vn_ref, h, d0, bd))

            def body(j, carry):
                s0 = pl.multiple_of((n_past - 1 - j) * bk, bk)
                out = []
                for h in range(hb):
                    dl, dacc = weigh(jnp.exp2(past_logits(h, s0) - ms[h]), vp_ref, h, s0, bk)
                    out += [carry[2 * h] + dl, carry[2 * h + 1] + dacc]
                return tuple(out)

            first_live = n_past
            for h in range(hb):
                row = pl.program_id(0) * n_heads + pl.program_id(1) * hb + h
                nearest = fend_ref[row, jnp.maximum(n_past - 1, 0)]
                for j in range(fend_ref.shape[1]):
                    d = (fend_ref[row, j] - nearest) * LOG2E + FOX_SKIP_BELOW
                    live = (d <= 0.0) | (4.0 * reach_sq[h] >= d * d)
                    first_live = jnp.where(live & (j < n_past), jnp.minimum(first_live, j), first_live)

            out = lax.fori_loop(0, n_past - first_live, body, tuple(state))
            finish(out[0::2], out[1::2])

        if has_past:
            online_walk()
        else:
            safe, reach_sq = None, []
            for h in range(hb):
                qf = q_ref[h, pl.ds(r0, bq), :].astype(F32)
                reach_sq.append(jnp.max(jnp.sum(qf * qf, axis=1, keepdims=True)) * ksq[h])
                ok = reach_sq[h] <= FOX_FIXED_RANGE ** 2
                safe = ok if safe is None else safe & ok
            pl.when(safe)(fixed_walk)
            pl.when(jnp.logical_not(safe))(online_walk)
        return 0

    ksq = None if has_past else [_max_sq_norm(kp_ref, h) for h in range(hb)]
    lax.fori_loop(0, tqs // bq, q_block, 0)


def _max_sq_norm(ref, h):
    rows = ref.shape[1]
    chunk = min(rows, FOX_BLOCK)

    def body(j, top):
        x = ref[h, pl.ds(pl.multiple_of(j * chunk, chunk), chunk), :].astype(F32)
        return jnp.maximum(top, jnp.max(jnp.sum(x * x, axis=1, keepdims=True)))

    return lax.fori_loop(0, rows // chunk, body, jnp.float32(0.0))


def _forgetting(q, kn, vn, f_cum, hb, past=None):
    bq = min(q.shape[2], FOX_BLOCK)
    nb, nh = q.shape[:2]
    body = functools.partial(_fox_kernel, bq=bq, bk=FOX_BLOCK, has_past=past is not None, n_heads=nh)
    smem = () if past is not None else (f_cum[:, :, 0, FOX_BLOCK - 1::FOX_BLOCK].reshape(nb * nh, -1),)
    return _attention_call(body, "forgetting", q, kn, vn, hb, (f_cum,), past, [], smem)


def _mem_kernel(q_ref, mk_ref, mv_ref, o_ref):
    for h in range(q_ref.shape[1] // HEAD_DIM):
        sl = slice(h * HEAD_DIM, (h + 1) * HEAD_DIM)
        z = lax.dot_general(q_ref[:, sl], mk_ref[:, sl].astype(BF16), _NT, preferred_element_type=F32)
        p = jnp.exp(z - jnp.max(z, axis=1, keepdims=True))
        o = jnp.dot(p.astype(BF16), mv_ref[:, sl].astype(BF16), preferred_element_type=F32)
        o_ref[:, sl] = (o / jnp.sum(p, axis=1, keepdims=True)).astype(o_ref.dtype)


def _memory_attention(q, mk, mv, lead=()):
    nb, t, w = q.shape
    tq = _tile(t, TOKEN_TILE)
    mem_spec = pl.BlockSpec((None,) * (len(lead) + 1) + mk.shape[-2:], lambda b, j: (*lead, b, 0, 0))
    return pl.pallas_call(
        _mem_kernel, grid=(nb, t // tq),
        in_specs=[pl.BlockSpec((None, tq, w), lambda b, j: (b, j, 0)), mem_spec, mem_spec],
        out_specs=pl.BlockSpec((None, tq, w), lambda b, j: (b, j, 0)),
        out_shape=jax.ShapeDtypeStruct(q.shape, BF16),
        compiler_params=_params("parallel", "parallel"), name="memory_attention",
    )(q, mk, mv)


def _merge_kernel(x_ref, o_sb, o_fx, o_mem, z_sb, z_fx, z_mem, g_sb, g_fx, g_mem,
                  w_sb, w_fx, w_mem, w_out, ln_g, ln_b, y_ref, ybf_ref, *, alpha):
    def branch(o_ref, z_ref, g_ref, w_ref):
        y = jnp.dot(o_ref[...] * z_ref[...], w_ref[...], preferred_element_type=F32)
        return g_ref[...].astype(F32) * y

    merged = branch(o_sb, z_sb, g_sb, w_sb) + branch(o_fx, z_fx, g_fx, w_fx) + branch(o_mem, z_mem, g_mem, w_mem)
    r = alpha * x_ref[...] + jnp.dot(merged.astype(BF16), w_out[...], preferred_element_type=F32)
    c = r - jnp.mean(r, axis=-1, keepdims=True)
    var = jnp.mean(c * c, axis=-1, keepdims=True)
    y = c * lax.rsqrt(var + LN_EPS) * ln_g[...] + ln_b[...]
    y_ref[...] = y
    ybf_ref[...] = y.astype(BF16)


def _merge(x, acts, weights, ln_g, ln_b, alpha):
    n, d = x.shape
    tm = _tile(n, TOKEN_TILE)
    tok = lambda a: pl.BlockSpec((tm, a.shape[1]), lambda i: (i, 0))
    full = lambda a: pl.BlockSpec(a.shape, lambda i: (0, 0))
    return pl.pallas_call(
        functools.partial(_merge_kernel, alpha=alpha), grid=(n // tm,),
        in_specs=[tok(x)] + [tok(a) for a in acts] + [full(w) for w in weights] + [full(ln_g), full(ln_b)],
        out_specs=[tok(x), tok(x)],
        out_shape=[jax.ShapeDtypeStruct((n, d), F32), jax.ShapeDtypeStruct((n, d), BF16)],
        compiler_params=_params("parallel"), name="merge",
    )(x, *acts, *weights, ln_g, ln_b)


def _head_group(n_heads, target):
    return max(g for g in range(1, min(n_heads, target) + 1) if n_heads % g == 0)


def kernel(x_prompt, x_sample, cache_sb_k, cache_sb_v, cache_fox_k, cache_fox_v, cache_fox_logf,
           cache_mem_k, cache_mem_v, mem_prompt, w_in, b_f, w_mem_kv, w_br_sb, w_br_fox, w_br_mem,
           w_out, ln_g, ln_b):
    depth, d, _ = w_in.shape
    nb, t, _ = x_prompt.shape
    nbs, ts, _ = x_sample.shape
    h_sb, h_fx, h_mem = cache_sb_k.shape[3], cache_fox_k.shape[3], cache_mem_k.shape[3]
    w_sb, w_fx, w_mem = h_sb * HEAD_DIM, h_fx * HEAD_DIM, h_mem * HEAD_DIM
    n_mem = mem_prompt.shape[1]
    alpha = float((2 * depth) ** 0.25)
    scale = HEAD_DIM ** -0.5
    ts_pad = -(-ts // (2 * SB_BLOCK)) * (2 * SB_BLOCK)

    sizes = [w_sb] * 4 + [w_fx] * 4 + [h_fx] + [w_mem] * 2 + [d] * 3
    offs = [0]
    for s in sizes:
        offs.append(offs[-1] + s)
    (c_qsb, c_ksb, c_vsb, c_zsb, c_qfx, c_kfx, c_vfx, c_zfx, c_f, c_qm, c_zm, c_gsb, c_gfx, c_gm) = [
        (offs[i], offs[i + 1]) for i in range(len(sizes))]

    w_in_bf = w_in.astype(BF16)
    w_mem_kv_bf = w_mem_kv.astype(BF16)
    w_br_bf = [w.astype(BF16) for w in (w_br_sb, w_br_fox, w_br_mem, w_out)]
    c_sb_k, c_sb_v, c_fx_k, c_fx_v = [jnp.transpose(c, (0, 1, 3, 2, 4))
                                      for c in (cache_sb_k, cache_sb_v, cache_fox_k, cache_fox_v)]
    c_mk, c_mv = [c.reshape(*c.shape[:3], -1) for c in (cache_mem_k, cache_mem_v)]
    c_logf = jnp.transpose(cache_fox_logf.astype(F32), (0, 1, 3, 2))
    mem_bf = mem_prompt.reshape(nb * n_mem, d).astype(BF16)

    xp, xs = x_prompt.reshape(nb * t, d), x_sample.reshape(nbs * ts, d)
    xp_bf, xs_bf = xp.astype(BF16), xs.astype(BF16)
    p_state = None
    p_rest = [[] for _ in range(3)]
    s_state = [[] for _ in range(5)]
    hb_sb_p, hb_fx_p = _head_group(h_sb, 6), _head_group(h_fx, 3)
    hb_sb_s, hb_fx_s = _head_group(h_sb, 3), _head_group(h_fx, 3)

    for l in range(depth):
        wl = lambda c: w_in_bf[l, :, c[0]:c[1]]
        w_kv = [wl(c) for c in (c_ksb, c_vsb, c_kfx, c_vfx)]
        w_qz = [wl(c) for c in (c_qsb, c_qfx, c_qm, c_zsb, c_zfx, c_zm)]
        w_qz.append(jnp.pad(wl(c_f), ((0, 0), (0, LANES - h_fx))))
        w_g = [wl(c) for c in (c_gsb, c_gfx, c_gm)]
        bf_row = jnp.pad(b_f[l].astype(F32), (0, LANES - h_fx)).reshape(1, LANES)
        merge_w = [w[l] for w in w_br_bf]
        g_row, b_row = ln_g[l].astype(F32).reshape(1, d), ln_b[l].astype(F32).reshape(1, d)

        kv = _project(xp_bf, nb, w_kv, ["kv"] * 4, scale, head_major=True, state=(depth, p_state), layer=l)
        p_state = kv[0::2]
        k_sb_bf, v_sb_bf, k_fx_bf, v_fx_bf = kv[1::2]
        q_sb, q_fx, q_m, z_sb, z_fx, z_m, logf = _project_qz(xp_bf, nb, w_qz, scale, bf_row, True)
        gates = _project(xp_bf, nb, w_g, ["sigmoid"] * 3, scale, head_major=True)
        f_cum = _cumsum_tokens(logf)[:, :h_fx, None, :]
        mk, mk_bf, mv, mv_bf = _project(mem_bf, nb, [w_mem_kv_bf[l, :, :w_mem], w_mem_kv_bf[l, :, w_mem:]],
                                        ["kv"] * 2, scale, head_major=False)
        o_sb = _stick_breaking(q_sb, k_sb_bf, v_sb_bf, hb_sb_p)
        o_fx = _forgetting(q_fx, k_fx_bf, v_fx_bf, f_cum, hb_fx_p)
        o_m = _memory_attention(q_m.reshape(nb, t, w_mem), mk_bf.reshape(nb, n_mem, w_mem),
                                mv_bf.reshape(nb, n_mem, w_mem))
        f2 = lambda a: a.reshape(nb * t, -1)
        xp, xp_bf = _merge(xp, [f2(o_sb), f2(o_fx), f2(o_m), z_sb, z_fx, z_m, *gates], merge_w, g_row, b_row, alpha)
        for lst, arr in zip(p_rest, (jnp.transpose(logf[:, :h_fx, :], (0, 2, 1)),
                                     mk.reshape(nb, n_mem, h_mem, HEAD_DIM), mv.reshape(nb, n_mem, h_mem, HEAD_DIM))):
            lst.append(arr)

        k_sb, k_sb_bf, v_sb, v_sb_bf, k_fx, k_fx_bf, v_fx, v_fx_bf = _project(
            xs_bf, 1, w_kv, ["kv"] * 4, scale, head_major=False)
        q_sb, q_fx, q_m, z_sb, z_fx, z_m, logf = _project_qz(xs_bf, 1, w_qz, scale, bf_row, False)
        gates = _project(xs_bf, 1, w_g, ["sigmoid"] * 3, scale, head_major=False)
        logf = jnp.transpose(logf[:, :h_fx].reshape(nbs, ts, h_fx), (0, 2, 1))
        f_all = jnp.concatenate([c_logf[l], jnp.pad(logf, ((0, 0), (0, 0), (0, ts_pad - ts)))], axis=-1)
        f_all = jnp.pad(f_all, ((0, 0), (0, SUBLANES - h_fx), (0, 0)))
        f_cum = _cumsum_tokens(f_all)[:, :h_fx, None, :]
        hm = lambda a: jnp.transpose(a.reshape(nbs, ts, -1, HEAD_DIM), (0, 2, 1, 3))
        hm_pad = lambda a: jnp.pad(hm(a), ((0, 0), (0, 0), (0, ts_pad - ts), (0, 0)))
        o_sb = _stick_breaking(hm(q_sb), hm_pad(k_sb_bf), hm_pad(v_sb_bf), hb_sb_s, past=(c_sb_k, c_sb_v, l))
        o_fx = _forgetting(hm(q_fx), hm_pad(k_fx_bf), hm_pad(v_fx_bf), f_cum, hb_fx_s, past=(c_fx_k, c_fx_v, l))
        o_m = _memory_attention(q_m.reshape(nbs, ts, w_mem), c_mk, c_mv, lead=(l,))
        f2 = lambda a: a.reshape(nbs * ts, -1)
        xs, xs_bf = _merge(xs, [f2(o_sb), f2(o_fx), f2(o_m), z_sb, z_fx, z_m, *gates], merge_w, g_row, b_row, alpha)
        for lst, arr in zip(s_state, (k_sb.reshape(nbs, ts, h_sb, HEAD_DIM), v_sb.reshape(nbs, ts, h_sb, HEAD_DIM),
                                      k_fx.reshape(nbs, ts, h_fx, HEAD_DIM), v_fx.reshape(nbs, ts, h_fx, HEAD_DIM),
                                      jnp.transpose(logf, (0, 2, 1)))):
            lst.append(arr)

    p_kv = [jnp.transpose(a, (0, 1, 3, 2, 4)) for a in p_state]
    p_out = [jnp.stack(a, axis=0) for a in p_rest]
    s_out = [jnp.stack(a, axis=0) for a in s_state]
    return (xp.reshape(nb, t, d), xs.reshape(nbs, ts, d), *p_kv, *p_out, *s_out)


def _project_qz(x_bf, nb, w_qz, scale, bf_row, head_major):
    kinds = ["q2", "q2", "q_tok", "silu", "silu", "silu", "logf"]
    return _project(x_bf, nb, w_qz, kinds, scale, head_major=head_major, bias=bf_row)
```

```python
import functools

import jax
import jax.numpy as jnp
from jax import lax
from jax.experimental import pallas as pl
from jax.experimental.pallas import tpu as pltpu

F32 = jnp.float32
BF16 = jnp.bfloat16
HEAD_DIM = 128
LANES = 128
SUBLANES = 8
LN_EPS = 1e-5
MASK_VALUE = -1e30
LOG2E = 1.4426950408889634
EXP2_ZERO_BELOW = -150.0
VMEM_LIMIT_BYTES = 56 * 1024 * 1024
TOKEN_TILE = 512
Q_SUPER = 1024
SB_BLOCK = 128
FOX_BLOCK = 512
FOX_FIXED_RANGE = 50.0
FOX_SKIP_BELOW = -152.0

_NT = (((1,), (1,)), ((), ()))


def _params(*sem):
    return pltpu.CompilerParams(dimension_semantics=sem, vmem_limit_bytes=VMEM_LIMIT_BYTES)


def _tile(n, target):
    for t in range(min(n, target), 0, -1):
        if n % t == 0 and t % SUBLANES == 0:
            return t
    return n


def _sigmoid(x):
    return 1.0 / (1.0 + jnp.exp(-x))


def _softplus(x):
    return jnp.maximum(x, 0.0) + jnp.log(1.0 + jnp.exp(-jnp.abs(x)))


def _proj_kernel(x_ref, *refs, kinds, scale, head_major, n_alias):
    n_seg = len(kinds)
    w_refs = refs[:n_seg]
    refs = refs[n_seg:]
    if "logf" in kinds:
        bf_ref, refs = refs[0], refs[1:]
    out_refs = refs[n_alias:]
    x = x_ref[...]
    oi = 0

    def put(o_ref, val):
        if head_major:
            for h in range(o_ref.shape[0]):
                o_ref[h] = val[:, h * HEAD_DIM:(h + 1) * HEAD_DIM].astype(o_ref.dtype)
        else:
            o_ref[...] = val.astype(o_ref.dtype)

    for w_ref, kind in zip(w_refs, kinds):
        acc = jnp.dot(x, w_ref[...], preferred_element_type=F32)
        if kind == "kv":
            put(out_refs[oi], acc)
            put(out_refs[oi + 1], acc)
            oi += 2
        elif kind in ("q", "q2"):
            put(out_refs[oi], acc * (scale * LOG2E if kind == "q2" else scale))
            oi += 1
        elif kind == "q_tok":
            out_refs[oi][...] = (acc * scale).astype(BF16)
            oi += 1
        elif kind == "logf":
            lf = -_softplus(-(acc + bf_ref[...]))
            out_refs[oi][...] = lf.T[:SUBLANES, :] if head_major else lf
            oi += 1
        else:
            out_refs[oi][...] = (acc * _sigmoid(acc) if kind == "silu" else _sigmoid(acc)).astype(BF16)
            oi += 1


def _project(x_bf, nb, weights, kinds, scale, *, head_major, bias=None, state=None, layer=0):
    n, d = x_bf.shape
    t = n // nb
    tm = _tile(t, TOKEN_TILE)
    tpb = t // tm
    const = lambda a: pl.BlockSpec(a.shape, lambda b, j: (0,) * a.ndim)
    tok = lambda w: pl.BlockSpec((tm, w), lambda b, j: (b * tpb + j, 0))
    in_specs = [tok(d)] + [const(w) for w in weights]
    args = [x_bf, *weights]
    if bias is not None:
        in_specs.append(const(bias))
        args.append(bias)
    out_shape, out_specs, aliases = [], [], {}
    for w, kind in zip(weights, kinds):
        width = w.shape[1]
        heads = width // HEAD_DIM
        if kind in ("kv", "q", "q2") and head_major:
            if kind == "kv":
                depth = state[0]
                out_shape.append(jax.ShapeDtypeStruct((depth, nb, heads, t, HEAD_DIM), F32))
                out_specs.append(pl.BlockSpec((None, None, heads, tm, HEAD_DIM),
                                              lambda b, j: (layer, b, 0, j, 0)))
            out_shape.append(jax.ShapeDtypeStruct((nb, heads, t, HEAD_DIM), BF16))
            out_specs.append(pl.BlockSpec((None, heads, tm, HEAD_DIM), lambda b, j: (b, 0, j, 0)))
        elif kind == "logf" and head_major:
            out_shape.append(jax.ShapeDtypeStruct((nb, SUBLANES, t), F32))
            out_specs.append(pl.BlockSpec((None, SUBLANES, tm), lambda b, j: (b, 0, j)))
        else:
            for dt in {"kv": (F32, BF16), "logf": (F32,)}.get(kind, (BF16,)):
                out_shape.append(jax.ShapeDtypeStruct((n, width), dt))
                out_specs.append(tok(width))
    bufs = [] if state is None or state[1] is None else list(state[1])
    if bufs:
        f32_outs = [i for i, s in enumerate(out_shape) if s.dtype == F32]
        for k, buf in enumerate(bufs):
            aliases[len(args)] = f32_outs[k]
            in_specs.append(pl.BlockSpec(memory_space=pl.ANY))
            args.append(buf)
    return pl.pallas_call(
        functools.partial(_proj_kernel, kinds=tuple(kinds), scale=scale, head_major=head_major,
                          n_alias=len(bufs)),
        grid=(nb, tpb), in_specs=in_specs, out_specs=out_specs, out_shape=out_shape,
        input_output_aliases=aliases,
        compiler_params=_params("parallel", "parallel"), name="project",
    )(*args)


def _cumsum_kernel(x_ref, o_ref):
    x = x_ref[...]
    lane = lax.broadcasted_iota(jnp.int32, x.shape, 1)
    shift = 1
    while shift < x.shape[1]:
        x = x + jnp.where(lane >= shift, pltpu.roll(x, shift, axis=1), 0.0)
        shift *= 2
    o_ref[...] = x


def _cumsum_tokens(logf):
    nb, h, t = logf.shape
    return pl.pallas_call(
        _cumsum_kernel, grid=(nb,),
        in_specs=[pl.BlockSpec((None, h, t), lambda b: (b, 0, 0))],
        out_specs=pl.BlockSpec((None, h, t), lambda b: (b, 0, 0)),
        out_shape=jax.ShapeDtypeStruct(logf.shape, F32),
        compiler_params=_params("parallel"), name="cumsum_tokens",
    )(logf)


def _attention_call(body, name, q, kn, vn, hb, extra, past, scratch, smem=()):
    nb, nh, tq, _ = q.shape
    tqs = min(tq, Q_SUPER)
    in_specs = [pl.BlockSpec((None, hb, tqs, HEAD_DIM), lambda b, g, s: (b, g, s, 0))]
    in_specs += [pl.BlockSpec((None, hb, kn.shape[2], HEAD_DIM), lambda b, g, s: (b, g, 0, 0))] * 2
    args = [q, kn, vn]
    for a in extra:
        in_specs.append(pl.BlockSpec((None, hb, 1, a.shape[-1]), lambda b, g, s: (b, g, 0, 0)))
        args.append(a)
    for a in smem:
        in_specs.append(pl.BlockSpec(memory_space=pltpu.SMEM))
        args.append(a)
    if past is not None:
        ck, cv, layer = past
        in_specs += [pl.BlockSpec((None, None, hb, ck.shape[3], HEAD_DIM),
                                  lambda b, g, s: (layer, b, g, 0, 0))] * 2
        args += [ck, cv]
    return pl.pallas_call(
        body, grid=(nb, nh // hb, tq // tqs), in_specs=in_specs,
        out_specs=pl.BlockSpec((None, tqs, hb * HEAD_DIM), lambda b, g, s: (b, s, g)),
        out_shape=jax.ShapeDtypeStruct((nb, tq, nh * HEAD_DIM), BF16),
        scratch_shapes=scratch,
        compiler_params=_params("parallel", "parallel", "arbitrary"), name=name,
    )(*args)


def _sb_step(q_ref, k_ref, v_ref, r0, k0, mask, first, scratch, bq):
    uu_ref, hl_ref, st_ref, ls_ref, acc_ref, r_ref = scratch
    hb = q_ref.shape[0]
    nk = 2 * SB_BLOCK
    for h in range(hb):
        z = lax.dot_general(q_ref[h, pl.ds(r0, bq), :], k_ref[h, pl.ds(k0, nk), :].astype(BF16), _NT,
                            preferred_element_type=F32)
        ls = jnp.minimum(z, 0.0) - jnp.log2(1.0 + jnp.exp2(-jnp.abs(z)))
        log1m = ls - z
        summed = log1m if mask is None else jnp.where(mask, log1m, 0.0)
        hi = summed.astype(BF16)
        lo = (summed - hi.astype(F32)).astype(BF16)
        for sub in range(2):
            rows = pl.ds((2 * h + sub) * bq, bq)
            hl_ref[rows, :SB_BLOCK] = hi[:, sub * SB_BLOCK:(sub + 1) * SB_BLOCK]
            hl_ref[rows, SB_BLOCK:] = lo[:, sub * SB_BLOCK:(sub + 1) * SB_BLOCK]
        ls_ref[h] = ls
    st_ref[...] = jnp.dot(hl_ref[...], uu_ref[...], preferred_element_type=F32)
    top = None
    for h in range(hb):
        s0 = st_ref[pl.ds((2 * h) * bq, bq), :]
        s1 = st_ref[pl.ds((2 * h + 1) * bq, bq), :]
        later = s1[:, SB_BLOCK:]
        tail1, tail0 = s1[:, :SB_BLOCK], s0[:, :SB_BLOCK] + later
        total = s0[:, SB_BLOCK:] + later
        if not first:
            carry = r_ref[h]
            tail1, tail0, total = tail1 + carry, tail0 + carry, total + carry
        w = jnp.exp2(ls_ref[h] + jnp.concatenate([tail0, tail1], axis=1))
        if mask is not None:
            w = jnp.where(mask, w, 0.0)
        pv = jnp.dot(w.astype(BF16), v_ref[h, pl.ds(k0, nk), :].astype(BF16), preferred_element_type=F32)
        if first:
            acc_ref[h] = pv
        else:
            acc_ref[h] += pv
        r_ref[h] = total
        top = total if top is None else jnp.maximum(top, total)
    return jnp.max(top)


def _sb_kernel(*refs, bq, has_past):
    if has_past:
        q_ref, kn_ref, vn_ref, kp_ref, vp_ref, o_ref, *scratch = refs
    else:
        q_ref, kn_ref, vn_ref, o_ref, *scratch = refs
        kp_ref, vp_ref = kn_ref, vn_ref
    uu_ref, acc_ref = scratch[0], scratch[4]
    hb, tqs, _ = q_ref.shape
    nk = 2 * SB_BLOCK
    base = pl.program_id(2) * tqs

    key = lax.broadcasted_iota(jnp.int32, uu_ref.shape, 0) % SB_BLOCK
    col = lax.broadcasted_iota(jnp.int32, uu_ref.shape, 1)
    uu_ref[...] = jnp.where((col >= SB_BLOCK) | (key > col), 1.0, 0.0).astype(BF16)
    strict = (lax.broadcasted_iota(jnp.int32, (bq, nk), 1) < lax.broadcasted_iota(jnp.int32, (bq, nk), 0))

    def q_block(i, _):
        r0 = pl.multiple_of(i * bq, bq)
        d0 = 0 if has_past else pl.multiple_of(base + r0, nk)
        top = _sb_step(q_ref, kn_ref, vn_ref, r0, d0, strict, True, scratch, bq)
        n_past = kp_ref.shape[1] // nk if has_past else (base + r0) // nk

        def cond(c):
            return (c[0] >= 0) & (c[1] > EXP2_ZERO_BELOW)

        def body(c):
            k0 = pl.multiple_of(c[0] * nk, nk)
            return c[0] - 1, _sb_step(q_ref, kp_ref, vp_ref, r0, k0, None, False, scratch, bq)

        lax.while_loop(cond, body, (n_past - 1, top))
        for h in range(hb):
            o_ref[pl.ds(r0, bq), h * HEAD_DIM:(h + 1) * HEAD_DIM] = acc_ref[h].astype(o_ref.dtype)
        return 0

    lax.fori_loop(0, tqs // bq, q_block, 0)


def _stick_breaking(q, kn, vn, hb, past=None):
    nk = 2 * SB_BLOCK
    bq = min(q.shape[2], nk)
    scratch = [pltpu.VMEM((nk, SB_BLOCK + LANES), BF16),
               pltpu.VMEM((2 * hb * bq, nk), BF16),
               pltpu.VMEM((2 * hb * bq, SB_BLOCK + LANES), F32),
               pltpu.VMEM((hb, bq, nk), F32),
               pltpu.VMEM((hb, bq, HEAD_DIM), F32),
               pltpu.VMEM((hb, bq, LANES), F32)]
    body = functools.partial(_sb_kernel, bq=bq, has_past=past is not None)
    return _attention_call(body, "stick_breaking", q, kn, vn, hb, (), past, scratch)


def _fox_kernel(*refs, bq, bk, has_past, n_heads):
    if has_past:
        q_ref, kn_ref, vn_ref, f_ref, kp_ref, vp_ref, o_ref = refs
        new_off = kp_ref.shape[1]
    else:
        q_ref, kn_ref, vn_ref, f_ref, fend_ref, o_ref, ksq_ref = refs
        kp_ref, vp_ref = kn_ref, vn_ref
        new_off = 0
    hb, tqs, _ = q_ref.shape
    bd = min(kn_ref.shape[1], bk)
    base = pl.program_id(2) * tqs
    causal = (lax.broadcasted_iota(jnp.int32, (bq, bd), 1) <= lax.broadcasted_iota(jnp.int32, (bq, bd), 0))

    def q_block(i, _):
        r0 = pl.multiple_of(i * bq, bq)
        d0 = 0 if has_past else pl.multiple_of(base + r0, bd)
        n_past = kp_ref.shape[1] // bk if has_past else (base + r0) // bk
        f_diag = [f_ref[h, :, pl.ds(new_off + d0, bd)] for h in range(hb)]
        f0s = [fd[:, 0:1] for fd in f_diag]
        frel = [(fd - f0) * LOG2E for fd, f0 in zip(f_diag, f0s)]

        def diag_logits(h):
            z = lax.dot_general(q_ref[h, pl.ds(r0, bq), :], kn_ref[h, pl.ds(d0, bd), :].astype(BF16), _NT,
                                preferred_element_type=F32)
            return z - frel[h]

        def past_logits(h, s0):
            z = lax.dot_general(q_ref[h, pl.ds(r0, bq), :], kp_ref[h, pl.ds(s0, bk), :].astype(BF16), _NT,
                                preferred_element_type=F32)
            return z - (f_ref[h, :, pl.ds(s0, bk)] - f0s[h]) * LOG2E

        def weigh(p, v_ref, h, s0, rows):
            return (jnp.sum(p, axis=1, keepdims=True),
                    jnp.dot(p.astype(BF16), v_ref[h, pl.ds(s0, rows), :].astype(BF16),
                            preferred_element_type=F32))

        def finish(ls, accs):
            for h in range(hb):
                o_ref[pl.ds(r0, bq), h * HEAD_DIM:(h + 1) * HEAD_DIM] = (accs[h] / ls[h]).astype(o_ref.dtype)

        def online_walk():
            state = []
            for h in range(hb):
                z = jnp.where(causal, diag_logits(h), MASK_VALUE)
                m = jnp.max(z, axis=1, keepdims=True)
                l, acc = weigh(jnp.exp2(z - m), vn_ref, h, d0, bd)
                state += [m, l, acc]

            def body(j, carry):
                s0 = pl.multiple_of((n_past - 1 - j) * bk, bk)
                out = []
                for h in range(hb):
                    m, l, acc = carry[3 * h:3 * h + 3]
                    z = past_logits(h, s0)
                    m_new = jnp.maximum(m, jnp.max(z, axis=1, keepdims=True))
                    a = jnp.exp2(m - m_new)
                    dl, dacc = weigh(jnp.exp2(z - m_new), vp_ref, h, s0, bk)
                    out += [m_new, a * l + dl, a * acc + dacc]
                return tuple(out)

            out = lax.fori_loop(0, n_past, body, tuple(state))
            finish(out[1::3], out[2::3])

        def fixed_walk():
            ms, state = [], []
            for h in range(hb):
                qf = q_ref[h, pl.ds(r0, bq), :].astype(F32)
                reach = jnp.sqrt(jnp.sum(qf * qf, axis=1, keepdims=True) * ksq[h])
                f_rows = jnp.transpose(jnp.broadcast_to(frel[h], (LANES, bd)))[:, 0:1]
                m = -f_rows - reach
                p = jnp.where(causal, jnp.exp2(diag_logits(h) - m), 0.0)
                ms.append(m)
                state += list(weigh(p, vn_ref, h, d0, bd))

            def body(j, carry):
                s0 = pl.multiple_of((n_past - 1 - j) * bk, bk)
                out = []
                for h in range(hb):
                    dl, dacc = weigh(jnp.exp2(past_logits(h, s0) - ms[h]), vp_ref, h, s0, bk)
                    out += [carry[2 * h] + dl, carry[2 * h + 1] + dacc]
                return tuple(out)

            first_live = n_past
            for h in range(hb):
                row = pl.program_id(0) * n_heads + pl.program_id(1) * hb + h
                nearest = fend_ref[row, jnp.maximum(n_past - 1, 0)]
                for j in range(fend_ref.shape[1]):
                    d = (fend_ref[row, j] - nearest) * LOG2E + FOX_SKIP_BELOW
                    live = (d <= 0.0) | (4.0 * reach_sq[h] >= d * d)
                    first_live = jnp.where(live & (j < n_past), jnp.minimum(first_live, j), first_live)

            out = lax.fori_loop(0, n_past - first_live, body, tuple(state))
            finish(out[0::2], out[1::2])

        if has_past:
            online_walk()
        else:
            safe, reach_sq = None, []
            for h in range(hb):
                qf = q_ref[h, pl.ds(r0, bq), :].astype(F32)
                reach_sq.append(jnp.max(jnp.sum(qf * qf, axis=1, keepdims=True)) * ksq[h])
                ok = reach_sq[h] <= FOX_FIXED_RANGE ** 2
                safe = ok if safe is None else safe & ok
            pl.when(safe)(fixed_walk)
            pl.when(jnp.logical_not(safe))(online_walk)
        return 0

    ksq = None
    if not has_past:
        @pl.when(pl.program_id(2) == 0)
        def _():
            for h in range(hb):
                ksq_ref[h] = _max_sq_norm(kp_ref, h)

        ksq = [ksq_ref[h] for h in range(hb)]
    lax.fori_loop(0, tqs // bq, q_block, 0)


def _max_sq_norm(ref, h):
    rows = ref.shape[1]
    chunk = min(rows, FOX_BLOCK)

    def body(j, top):
        x = ref[h, pl.ds(pl.multiple_of(j * chunk, chunk), chunk), :].astype(F32)
        return jnp.maximum(top, jnp.max(jnp.sum(x * x, axis=1, keepdims=True)))

    return lax.fori_loop(0, rows // chunk, body, jnp.float32(0.0))


def _forgetting(q, kn, vn, f_cum, hb, past=None):
    bq = min(q.shape[2], FOX_BLOCK)
    nb, nh = q.shape[:2]
    body = functools.partial(_fox_kernel, bq=bq, bk=FOX_BLOCK, has_past=past is not None, n_heads=nh)
    smem = () if past is not None else (f_cum[:, :, 0, FOX_BLOCK - 1::FOX_BLOCK].reshape(nb * nh, -1),)
    scratch = [] if past is not None else [pltpu.SMEM((hb,), F32)]
    return _attention_call(body, "forgetting", q, kn, vn, hb, (f_cum,), past, scratch, smem)


def _mem_kernel(q_ref, mk_ref, mv_ref, o_ref):
    for h in range(q_ref.shape[1] // HEAD_DIM):
        sl = slice(h * HEAD_DIM, (h + 1) * HEAD_DIM)
        z = lax.dot_general(q_ref[:, sl], mk_ref[:, sl].astype(BF16), _NT, preferred_element_type=F32)
        p = jnp.exp(z - jnp.max(z, axis=1, keepdims=True))
        o = jnp.dot(p.astype(BF16), mv_ref[:, sl].astype(BF16), preferred_element_type=F32)
        o_ref[:, sl] = (o / jnp.sum(p, axis=1, keepdims=True)).astype(o_ref.dtype)


def _memory_attention(q, mk, mv, lead=()):
    nb, t, w = q.shape
    tq = _tile(t, TOKEN_TILE)
    mem_spec = pl.BlockSpec((None,) * (len(lead) + 1) + mk.shape[-2:], lambda b, j: (*lead, b, 0, 0))
    return pl.pallas_call(
        _mem_kernel, grid=(nb, t // tq),
        in_specs=[pl.BlockSpec((None, tq, w), lambda b, j: (b, j, 0)), mem_spec, mem_spec],
        out_specs=pl.BlockSpec((None, tq, w), lambda b, j: (b, j, 0)),
        out_shape=jax.ShapeDtypeStruct(q.shape, BF16),
        compiler_params=_params("parallel", "parallel"), name="memory_attention",
    )(q, mk, mv)


def _merge_kernel(x_ref, o_sb, o_fx, o_mem, z_sb, z_fx, z_mem, g_sb, g_fx, g_mem,
                  w_sb, w_fx, w_mem, w_out, ln_g, ln_b, y_ref, ybf_ref, *, alpha):
    def branch(o_ref, z_ref, g_ref, w_ref):
        y = jnp.dot(o_ref[...] * z_ref[...], w_ref[...], preferred_element_type=F32)
        return g_ref[...].astype(F32) * y

    merged = branch(o_sb, z_sb, g_sb, w_sb) + branch(o_fx, z_fx, g_fx, w_fx) + branch(o_mem, z_mem, g_mem, w_mem)
    r = alpha * x_ref[...] + jnp.dot(merged.astype(BF16), w_out[...], preferred_element_type=F32)
    c = r - jnp.mean(r, axis=-1, keepdims=True)
    var = jnp.mean(c * c, axis=-1, keepdims=True)
    y = c * lax.rsqrt(var + LN_EPS) * ln_g[...] + ln_b[...]
    y_ref[...] = y
    ybf_ref[...] = y.astype(BF16)


def _merge(x, acts, weights, ln_g, ln_b, alpha):
    n, d = x.shape
    tm = _tile(n, TOKEN_TILE)
    tok = lambda a: pl.BlockSpec((tm, a.shape[1]), lambda i: (i, 0))
    full = lambda a: pl.BlockSpec(a.shape, lambda i: (0, 0))
    return pl.pallas_call(
        functools.partial(_merge_kernel, alpha=alpha), grid=(n // tm,),
        in_specs=[tok(x)] + [tok(a) for a in acts] + [full(w) for w in weights] + [full(ln_g), full(ln_b)],
        out_specs=[tok(x), tok(x)],
        out_shape=[jax.ShapeDtypeStruct((n, d), F32), jax.ShapeDtypeStruct((n, d), BF16)],
        compiler_params=_params("parallel"), name="merge",
    )(x, *acts, *weights, ln_g, ln_b)


def _head_group(n_heads, target):
    return max(g for g in range(1, min(n_heads, target) + 1) if n_heads % g == 0)


def kernel(x_prompt, x_sample, cache_sb_k, cache_sb_v, cache_fox_k, cache_fox_v, cache_fox_logf,
           cache_mem_k, cache_mem_v, mem_prompt, w_in, b_f, w_mem_kv, w_br_sb, w_br_fox, w_br_mem,
           w_out, ln_g, ln_b):
    depth, d, _ = w_in.shape
    nb, t, _ = x_prompt.shape
    nbs, ts, _ = x_sample.shape
    h_sb, h_fx, h_mem = cache_sb_k.shape[3], cache_fox_k.shape[3], cache_mem_k.shape[3]
    w_sb, w_fx, w_mem = h_sb * HEAD_DIM, h_fx * HEAD_DIM, h_mem * HEAD_DIM
    n_mem = mem_prompt.shape[1]
    alpha = float((2 * depth) ** 0.25)
    scale = HEAD_DIM ** -0.5
    ts_pad = -(-ts // (2 * SB_BLOCK)) * (2 * SB_BLOCK)

    sizes = [w_sb] * 4 + [w_fx] * 4 + [h_fx] + [w_mem] * 2 + [d] * 3
    offs = [0]
    for s in sizes:
        offs.append(offs[-1] + s)
    (c_qsb, c_ksb, c_vsb, c_zsb, c_qfx, c_kfx, c_vfx, c_zfx, c_f, c_qm, c_zm, c_gsb, c_gfx, c_gm) = [
        (offs[i], offs[i + 1]) for i in range(len(sizes))]

    w_in_bf = w_in.astype(BF16)
    w_mem_kv_bf = w_mem_kv.astype(BF16)
    w_br_bf = [w.astype(BF16) for w in (w_br_sb, w_br_fox, w_br_mem, w_out)]
    c_sb_k, c_sb_v, c_fx_k, c_fx_v = [jnp.transpose(c, (0, 1, 3, 2, 4))
                                      for c in (cache_sb_k, cache_sb_v, cache_fox_k, cache_fox_v)]
    c_mk, c_mv = [c.reshape(*c.shape[:3], -1) for c in (cache_mem_k, cache_mem_v)]
    c_logf = jnp.transpose(cache_fox_logf.astype(F32), (0, 1, 3, 2))
    mem_bf = mem_prompt.reshape(nb * n_mem, d).astype(BF16)

    xp, xs = x_prompt.reshape(nb * t, d), x_sample.reshape(nbs * ts, d)
    xp_bf, xs_bf = xp.astype(BF16), xs.astype(BF16)
    p_state = None
    p_rest = [[] for _ in range(3)]
    s_state = [[] for _ in range(5)]
    hb_sb_p, hb_fx_p = _head_group(h_sb, 6), _head_group(h_fx, 3)
    hb_sb_s, hb_fx_s = _head_group(h_sb, 3), _head_group(h_fx, 3)

    for l in range(depth):
        wl = lambda c: w_in_bf[l, :, c[0]:c[1]]
        w_kv = [wl(c) for c in (c_ksb, c_vsb, c_kfx, c_vfx)]
        w_qz = [wl(c) for c in (c_qsb, c_qfx, c_qm, c_zsb, c_zfx, c_zm)]
        w_qz.append(jnp.pad(wl(c_f), ((0, 0), (0, LANES - h_fx))))
        w_g = [wl(c) for c in (c_gsb, c_gfx, c_gm)]
        bf_row = jnp.pad(b_f[l].astype(F32), (0, LANES - h_fx)).reshape(1, LANES)
        merge_w = [w[l] for w in w_br_bf]
        g_row, b_row = ln_g[l].astype(F32).reshape(1, d), ln_b[l].astype(F32).reshape(1, d)

        kv = _project(xp_bf, nb, w_kv, ["kv"] * 4, scale, head_major=True, state=(depth, p_state), layer=l)
        p_state = kv[0::2]
        k_sb_bf, v_sb_bf, k_fx_bf, v_fx_bf = kv[1::2]
        q_sb, q_fx, q_m, z_sb, z_fx, z_m, logf = _project_qz(xp_bf, nb, w_qz, scale, bf_row, True)
        gates = _project(xp_bf, nb, w_g, ["sigmoid"] * 3, scale, head_major=True)
        f_cum = _cumsum_tokens(logf)[:, :h_fx, None, :]
        mk, mk_bf, mv, mv_bf = _project(mem_bf, nb, [w_mem_kv_bf[l, :, :w_mem], w_mem_kv_bf[l, :, w_mem:]],
                                        ["kv"] * 2, scale, head_major=False)
        o_sb = _stick_breaking(q_sb, k_sb_bf, v_sb_bf, hb_sb_p)
        o_fx = _forgetting(q_fx, k_fx_bf, v_fx_bf, f_cum, hb_fx_p)
        o_m = _memory_attention(q_m.reshape(nb, t, w_mem), mk_bf.reshape(nb, n_mem, w_mem),
                                mv_bf.reshape(nb, n_mem, w_mem))
        f2 = lambda a: a.reshape(nb * t, -1)
        xp, xp_bf = _merge(xp, [f2(o_sb), f2(o_fx), f2(o_m), z_sb, z_fx, z_m, *gates], merge_w, g_row, b_row, alpha)
        for lst, arr in zip(p_rest, (jnp.transpose(logf[:, :h_fx, :], (0, 2, 1)),
                                     mk.reshape(nb, n_mem, h_mem, HEAD_DIM), mv.reshape(nb, n_mem, h_mem, HEAD_DIM))):
            lst.append(arr)

        k_sb, k_sb_bf, v_sb, v_sb_bf, k_fx, k_fx_bf, v_fx, v_fx_bf = _project(
            xs_bf, 1, w_kv, ["kv"] * 4, scale, head_major=False)
        q_sb, q_fx, q_m, z_sb, z_fx, z_m, logf = _project_qz(xs_bf, 1, w_qz, scale, bf_row, False)
        gates = _project(xs_bf, 1, w_g, ["sigmoid"] * 3, scale, head_major=False)
        logf = jnp.transpose(logf[:, :h_fx].reshape(nbs, ts, h_fx), (0, 2, 1))
        f_all = jnp.concatenate([c_logf[l], jnp.pad(logf, ((0, 0), (0, 0), (0, ts_pad - ts)))], axis=-1)
        f_all = jnp.pad(f_all, ((0, 0), (0, SUBLANES - h_fx), (0, 0)))
        f_cum = _cumsum_tokens(f_all)[:, :h_fx, None, :]
        hm = lambda a: jnp.transpose(a.reshape(nbs, ts, -1, HEAD_DIM), (0, 2, 1, 3))
        hm_pad = lambda a: jnp.pad(hm(a), ((0, 0), (0, 0), (0, ts_pad - ts), (0, 0)))
        o_sb = _stick_breaking(hm(q_sb), hm_pad(k_sb_bf), hm_pad(v_sb_bf), hb_sb_s, past=(c_sb_k, c_sb_v, l))
        o_fx = _forgetting(hm(q_fx), hm_pad(k_fx_bf), hm_pad(v_fx_bf), f_cum, hb_fx_s, past=(c_fx_k, c_fx_v, l))
        o_m = _memory_attention(q_m.reshape(nbs, ts, w_mem), c_mk, c_mv, lead=(l,))
        f2 = lambda a: a.reshape(nbs * ts, -1)
        xs, xs_bf = _merge(xs, [f2(o_sb), f2(o_fx), f2(o_m), z_sb, z_fx, z_m, *gates], merge_w, g_row, b_row, alpha)
        for lst, arr in zip(s_state, (k_sb.reshape(nbs, ts, h_sb, HEAD_DIM), v_sb.reshape(nbs, ts, h_sb, HEAD_DIM),
                                      k_fx.reshape(nbs, ts, h_fx, HEAD_DIM), v_fx.reshape(nbs, ts, h_fx, HEAD_DIM),
                                      jnp.transpose(logf, (0, 2, 1)))):
            lst.append(arr)

    p_kv = [jnp.transpose(a, (0, 1, 3, 2, 4)) for a in p_state]
    p_out = [jnp.stack(a, axis=0) for a in p_rest]
    s_out = [jnp.stack(a, axis=0) for a in s_state]
    return (xp.reshape(nb, t, d), xs.reshape(nbs, ts, d), *p_kv, *p_out, *s_out)


def _project_qz(x_bf, nb, w_qz, scale, bf_row, head_major):
    kinds = ["q2", "q2", "q_tok", "silu", "silu", "silu", "logf"]
    return _project(x_bf, nb, w_qz, kinds, scale, head_major=head_major, bias=bf_row)
```
